```python
import jax
import jax.numpy as jnp
from jax import lax
import numpy as np

D_MODEL = 4096
BATCH = 8
SEQ = 2048
DEPTH = 2

CTX_LEN = 256
GRID_W = 64
HEAD_DIM = 128
ROPE_BASE = 10000.0
EPS = 1e-6
NEG_INF = -1e30
F32 = jnp.float32
N_BRANCHES = 4
BRANCH_WIDTH = 1024

ATT_Q_HEADS = 8
ATT_KV_HEADS = 2
ATT_GROUP = ATT_Q_HEADS // ATT_KV_HEADS
WINDOW = 128
ATT_BLOCK = 128
DN_HEADS = 8
DN_WIDTH = DN_HEADS * HEAD_DIM
DN_CONV = 5
DN_CHUNK = 64
FT_GROUPS = 4
FT_GROUP_DIM = BRANCH_WIDTH // FT_GROUPS
POOL_WINDOWS = (2, 4, 8, 16)
POOL_GROUP_DIM = BRANCH_WIDTH // len(POOL_WINDOWS)
N_EXPERTS = 64
N_EXPERT_GROUPS = 8
TOPK_GROUPS = 4
TOP_K = 8
EXPERT_DIM = 256
SHARED_DIM = 512
ROUTE_SCALE = 2.5

ATT_KV_W = ATT_KV_HEADS * HEAD_DIM
ATT_Q_W = ATT_Q_HEADS * HEAD_DIM
CTX_SPLITS = (ATT_KV_W, ATT_KV_W, DN_WIDTH, DN_WIDTH, 2 * DN_HEADS, 2 * DN_HEADS)
Q_SPLITS = (ATT_Q_W, DN_WIDTH, DN_WIDTH, BRANCH_WIDTH, BRANCH_WIDTH, N_BRANCHES * D_MODEL)
COLS_CTX = sum(CTX_SPLITS)
COLS_TOTAL = COLS_CTX + sum(Q_SPLITS)

kernel_name = 'hybrid_fnet_deltanet_pool_swa_moe_dit'


def split_cols(t, widths):
    outs = []
    off = 0
    for w in widths:
        outs.append(t[..., off:off + w])
        off += w
    return outs


def rms_norm(x, g):
    xf = x.astype(F32)
    y = xf * lax.rsqrt(jnp.mean(xf * xf, axis=-1, keepdims=True) + EPS)
    return (y * g.astype(F32)).astype(x.dtype)


def l2_normalize(x):
    xf = x.astype(F32)
    return xf * lax.rsqrt(jnp.sum(xf * xf, axis=-1, keepdims=True) + EPS)


def centred_depthwise_conv(x, w):
    k = w.shape[0]
    return lax.conv_general_dilated(
        x, w[:, None, :].astype(x.dtype), window_strides=(1,),
        padding=[(k // 2, k // 2)], dimension_numbers=('NWC', 'WIO', 'NWC'),
        feature_group_count=x.shape[-1])


def axial_rope_angles(n):
    rows = n // GRID_W
    row = jnp.repeat(jnp.arange(rows, dtype=F32), GRID_W)
    col = jnp.tile(jnp.arange(GRID_W, dtype=F32), rows)
    half = HEAD_DIM // 2
    inv_freq = ROPE_BASE ** (-jnp.arange(0, half, 2, dtype=F32) / half)
    return row[:, None] * inv_freq, col[:, None] * inv_freq


def _rotate(x, ang):
    m = ang.shape[-1]
    cos = jnp.cos(ang)[None, :, None, :]
    sin = jnp.sin(ang)[None, :, None, :]
    x1, x2 = x[..., :m], x[..., m:]
    return jnp.concatenate([x1 * cos - x2 * sin, x2 * cos + x1 * sin], axis=-1)


def apply_axial_rope(x, ang_row, ang_col):
    half = HEAD_DIM // 2
    xf = x.astype(F32)
    y = jnp.concatenate([_rotate(xf[..., :half], ang_row), _rotate(xf[..., half:], ang_col)], axis=-1)
    return y.astype(x.dtype)


def latent_window_attention(q, k, v, kc, vc, sink):
    b, t = q.shape[:2]
    nb = t // ATT_BLOCK
    scale = HEAD_DIM ** -0.5
    qb = q.reshape(b, nb, ATT_BLOCK, ATT_KV_HEADS, ATT_GROUP, HEAD_DIM)

    def band(z):
        zp = jnp.pad(z, ((0, 0), (ATT_BLOCK, ATT_BLOCK), (0, 0), (0, 0)))
        zp = zp.reshape(b, nb + 2, ATT_BLOCK, ATT_KV_HEADS, HEAD_DIM)
        return jnp.concatenate([zp[:, :-2], zp[:, 1:-1], zp[:, 2:]], axis=2)

    kb, vb = band(k), band(v)
    qpos = jnp.arange(nb)[:, None] * ATT_BLOCK + jnp.arange(ATT_BLOCK)[None, :]
    kpos = (jnp.arange(nb)[:, None] - 1) * ATT_BLOCK + jnp.arange(3 * ATT_BLOCK)[None, :]
    rel = kpos[:, None, :] - qpos[:, :, None]
    valid = (jnp.abs(rel) <= WINDOW) & (kpos[:, None, :] >= 0) & (kpos[:, None, :] < t)
    s_loc = jnp.einsum('bnqhgd,bnkhd->bnhgqk', qb, kb).astype(F32) * scale
    s_loc = jnp.where(valid[None, :, None, None], s_loc, NEG_INF)
    s_ctx = jnp.einsum('bnqhgd,blhd->bnhgql', qb, kc).astype(F32) * scale
    s_sink = jnp.broadcast_to(sink.astype(F32).reshape(ATT_KV_HEADS, ATT_GROUP)[None, None, :, :, None, None],
                              s_ctx.shape[:-1] + (1,))
    p = jax.nn.softmax(jnp.concatenate([s_loc, s_ctx, s_sink], axis=-1), axis=-1)
    nk = 3 * ATT_BLOCK
    n_ctx = kc.shape[1]
    p_loc = p[..., :nk].astype(v.dtype)
    p_ctx = p[..., nk:nk + n_ctx].astype(v.dtype)
    o = jnp.einsum('bnhgqk,bnkhd->bnqhgd', p_loc, vb) + jnp.einsum('bnhgql,blhd->bnqhgd', p_ctx, vc)
    return o.reshape(b, t, ATT_Q_W)


def context_attention(qc, kc, vc, sink):
    b, l = qc.shape[:2]
    scale = HEAD_DIM ** -0.5
    qg = qc.reshape(b, l, ATT_KV_HEADS, ATT_GROUP, HEAD_DIM)
    s = jnp.einsum('blhgd,bmhd->bhglm', qg, kc).astype(F32) * scale
    s_sink = jnp.broadcast_to(sink.astype(F32).reshape(ATT_KV_HEADS, ATT_GROUP)[None, :, :, None, None],
                              s.shape[:-1] + (1,))
    p = jax.nn.softmax(jnp.concatenate([s, s_sink], axis=-1), axis=-1)[..., :l].astype(vc.dtype)
    o = jnp.einsum('bhglm,bmhd->blhgd', p, vc)
    return o.reshape(b, l, ATT_Q_W)


def gated_delta_chunks(q, k, v, g, beta, s0):
    b, t, h, dk = k.shape
    dv = v.shape[-1]
    n = t // DN_CHUNK
    c = DN_CHUNK

    def chunked(z):
        z = z.reshape((b, n, c, h) + z.shape[3:])
        return jnp.moveaxis(z, (1, 3), (0, 2))

    kc_, vc_ = chunked(k), chunked(v)
    gc, bc = chunked(g), chunked(beta)
    gcum = jnp.cumsum(gc, axis=-1)
    idx = jnp.arange(c)
    incl = idx[:, None] >= idx[None, :]
    strict = idx[:, None] > idx[None, :]
    decay = jnp.exp(jnp.where(incl, gcum[..., :, None] - gcum[..., None, :], -jnp.inf))
    kk = jnp.einsum('nbhcd,nbhjd->nbhcj', kc_, kc_)
    a_mat = jnp.where(strict, bc[..., :, None] * kk * decay, 0.0) + jnp.eye(c, dtype=F32)
    rhs = jnp.concatenate([vc_ * bc[..., None], kc_ * (bc * jnp.exp(gcum))[..., None]], axis=-1)
    sol = lax.linalg.triangular_solve(a_mat, rhs, left_side=True, lower=True, unit_diagonal=True)
    u, w = sol[..., :dv], sol[..., dv:]
    k_dec = kc_ * jnp.exp(gcum[..., -1:] - gcum)[..., None]
    g_last = jnp.exp(gcum[..., -1])

    if q is None:
        def step_state(s, xs):
            u_i, w_i, kd_i, gl_i = xs
            v_new = u_i - jnp.einsum('bhck,bhkv->bhcv', w_i, s)
            s = gl_i[..., None, None] * s + jnp.einsum('bhck,bhcv->bhkv', kd_i, v_new)
            return s, None
        s_fin, _ = lax.scan(step_state, s0, (u, w, k_dec, g_last))
        return None, s_fin

    qc_ = chunked(q)
    q_dec = qc_ * jnp.exp(gcum)[..., None]
    qk = jnp.einsum('nbhcd,nbhjd->nbhcj', qc_, kc_) * decay

    def step(s, xs):
        u_i, w_i, kd_i, gl_i, qd_i, qk_i = xs
        v_new = u_i - jnp.einsum('bhck,bhkv->bhcv', w_i, s)
        o_i = jnp.einsum('bhck,bhkv->bhcv', qd_i, s) + jnp.einsum('bhcj,bhjv->bhcv', qk_i, v_new)
        s = gl_i[..., None, None] * s + jnp.einsum('bhck,bhcv->bhkv', kd_i, v_new)
        return s, o_i

    s_fin, o = lax.scan(step, s0, (u, w, k_dec, g_last, q_dec, qk))
    o = jnp.moveaxis(o, (0, 2), (1, 3)).reshape(b, t, h, dv)
    return o, s_fin


def gdn_qkv(q_raw, k_raw, v_raw, conv_w):
    b, t, _ = k_raw.shape
    kv = jax.nn.silu(centred_depthwise_conv(jnp.concatenate([k_raw, v_raw], axis=-1), conv_w[:, DN_WIDTH:]))
    k = l2_normalize(kv[..., :DN_WIDTH].reshape(b, t, DN_HEADS, HEAD_DIM))
    v = kv[..., DN_WIDTH:].reshape(b, t, DN_HEADS, HEAD_DIM).astype(F32)
    q = None
    if q_raw is not None:
        qq = jax.nn.silu(centred_depthwise_conv(q_raw, conv_w[:, :DN_WIDTH]))
        q = l2_normalize(qq.reshape(b, t, DN_HEADS, HEAD_DIM)) * (HEAD_DIM ** -0.5)
    return q, k, v


def gdn_bidirectional(q, k, v, a, b, a_log, dt_bias, init_states):
    a = a.astype(F32)
    b = b.astype(F32)
    out = None
    states = []
    for d in range(2):
        hs = slice(d * DN_HEADS, (d + 1) * DN_HEADS)
        g = -jnp.exp(a_log[d].astype(F32)) * jax.nn.softplus(a[..., hs] + dt_bias[d].astype(F32))
        beta = jax.nn.sigmoid(b[..., hs])
        seq = (q, k, v, g, beta)
        if d == 1:
            seq = tuple(None if z is None else jnp.flip(z, axis=1) for z in seq)
        o, s = gated_delta_chunks(seq[0], seq[1], seq[2], seq[3], seq[4], init_states[d])
        states.append(s)
        if o is not None:
            o = jnp.flip(o, axis=1) if d == 1 else o
            out = o if out is None else out + o
    return out, states


def gdn_output(o, z, norm_w):
    b, t = z.shape[:2]
    on = o * lax.rsqrt(jnp.mean(o * o, axis=-1, keepdims=True) + EPS) * norm_w.astype(F32)
    return (on.reshape(b, t, DN_WIDTH) * jax.nn.silu(z.astype(F32))).astype(z.dtype)


def fourier_branch(u):
    b, t, _ = u.shape
    ug = u.astype(F32).reshape(b, t, FT_GROUPS, FT_GROUP_DIM)
    f = jnp.fft.fft2(ug, axes=(1, 3), norm='ortho').real
    return f.reshape(b, t, BRANCH_WIDTH).astype(u.dtype)


def pool_branch(u, pool_w, pool_scale):
    b, t, _ = u.shape
    uf = u.astype(F32).reshape(b, t, len(POOL_WINDOWS), POOL_GROUP_DIM)
    cs = jnp.concatenate([jnp.zeros((b, 1) + uf.shape[2:], F32), jnp.cumsum(uf, axis=1)], axis=1)
    pos = jnp.arange(t)
    outs = []
    for gi, win in enumerate(POOL_WINDOWS):
        lo = jnp.clip(pos - win // 2, 0, t)
        hi = jnp.clip(pos + win // 2, 0, t)
        csg = cs[:, :, gi]
        mean = (csg[:, hi] - csg[:, lo]) / (hi - lo).astype(F32)[None, :, None]
        outs.append(mean - uf[:, :, gi])
    m = jnp.stack(outs, axis=2)
    y = jnp.einsum('btgc,gcd->btgd', m, pool_w.astype(F32)).reshape(b, t, BRANCH_WIDTH)
    return (y * pool_scale.astype(F32)).astype(u.dtype)


def merge_branches(branches, gate_logits, w_branch, w_out):
    b, t, _ = gate_logits.shape
    gl = gate_logits.reshape(b, t, N_BRANCHES, D_MODEL)
    acc = None
    for n, y in enumerate(branches):
        term = jax.nn.sigmoid(gl[:, :, n]) * (y @ w_branch[n])
        acc = term if acc is None else acc + term
    return acc @ w_out


def swiglu(h, w_gate, w_up, w_down):
    return (jax.nn.silu(h @ w_gate) * (h @ w_up)) @ w_down


def moe_ffn(hf, p):
    n = hf.shape[0]
    per_group = N_EXPERTS // N_EXPERT_GROUPS
    scores = jax.nn.sigmoid((hf @ p['router_w']).astype(F32))
    biased = scores + p['router_bias'].astype(F32)
    grp_score = lax.top_k(biased.reshape(n, N_EXPERT_GROUPS, per_group), 2)[0].sum(-1)
    _, top_groups = lax.top_k(grp_score, TOPK_GROUPS)
    group_mask = jax.nn.one_hot(top_groups, N_EXPERT_GROUPS, dtype=F32).sum(1)
    expert_mask = jnp.repeat(group_mask, per_group, axis=-1) > 0
    _, top_experts = lax.top_k(jnp.where(expert_mask, biased, -jnp.inf), TOP_K)
    w = jnp.take_along_axis(scores, top_experts, axis=-1)
    w = w / jnp.sum(w, axis=-1, keepdims=True) * ROUTE_SCALE
    gates = jnp.einsum('nk,nke->ne', w, jax.nn.one_hot(top_experts, N_EXPERTS, dtype=F32)).astype(hf.dtype)
    y = swiglu(hf, p['shared_gate'], p['shared_up'], p['shared_down'])
    for blk in range(N_EXPERT_GROUPS):
        sl = slice(blk * per_group, (blk + 1) * per_group)
        a = jnp.einsum('nd,edi->nei', hf, p['exp_gate'][sl])
        u = jnp.einsum('nd,edi->nei', hf, p['exp_up'][sl])
        y = y + jnp.einsum('nei,eid->nd', jax.nn.silu(a) * u * gates[:, sl, None], p['exp_down'][sl])
    return y


def layer_forward(x, xc, mod, mod_c, p, ang_row, ang_col, last):
    b, t, _ = x.shape
    l = xc.shape[1]
    sh1, sc1, g1, sh2, sc2, g2 = jnp.split(mod[:, None, :], 6, axis=-1)
    csh1, csc1, cg1, csh2, csc2, cg2 = jnp.split(mod_c, 6, axis=-1)
    zero_state = jnp.zeros((b, DN_HEADS, HEAD_DIM, HEAD_DIM), F32)

    hc = rms_norm(xc, p['norm_mix']) * (1 + csc1) + csh1
    pc = hc @ (p['w_in'][:, :COLS_CTX] if last else p['w_in'])
    ck_a, cv_a, ck_d, cv_d, ca_d, cb_d = split_cols(pc, CTX_SPLITS)
    kc = ck_a.reshape(b, l, ATT_KV_HEADS, HEAD_DIM)
    vc = cv_a.reshape(b, l, ATT_KV_HEADS, HEAD_DIM)
    if last:
        _, ckd, cvd = gdn_qkv(None, ck_d, cv_d, p['dn_conv'])
        _, cstates = gdn_bidirectional(None, ckd, cvd, ca_d, cb_d, p['dn_a_log'], p['dn_dt_bias'],
                                       (zero_state, zero_state))
    else:
        cq_a, cq_d, cz_d, c_ft, c_pool, c_gate = split_cols(pc[..., COLS_CTX:], Q_SPLITS)
        cqd, ckd, cvd = gdn_qkv(cq_d, ck_d, cv_d, p['dn_conv'])
        co_d, cstates = gdn_bidirectional(cqd, ckd, cvd, ca_d, cb_d, p['dn_a_log'], p['dn_dt_bias'],
                                          (zero_state, zero_state))
        y_attn_c = context_attention(cq_a.reshape(b, l, ATT_Q_HEADS, HEAD_DIM), kc, vc, p['attn_sink'])
        branches_c = (fourier_branch(c_ft), gdn_output(co_d, cz_d, p['dn_norm']),
                      pool_branch(c_pool, p['pool_w'], p['pool_scale']), y_attn_c)
        xc = xc + cg1 * merge_branches(branches_c, c_gate, p['w_branch'], p['w_out'])

    hl = rms_norm(x, p['norm_mix']) * (1 + sc1) + sh1
    pl = hl @ p['w_in']
    lk_a, lv_a, lk_d, lv_d, la_d, lb_d = split_cols(pl, CTX_SPLITS)
    lq_a, lq_d, lz_d, l_ft, l_pool, l_gate = split_cols(pl[..., COLS_CTX:], Q_SPLITS)
    q_att = apply_axial_rope(lq_a.reshape(b, t, ATT_Q_HEADS, HEAD_DIM), ang_row, ang_col)
    k_att = apply_axial_rope(lk_a.reshape(b, t, ATT_KV_HEADS, HEAD_DIM), ang_row, ang_col)
    v_att = lv_a.reshape(b, t, ATT_KV_HEADS, HEAD_DIM)
    y_attn = latent_window_attention(q_att, k_att, v_att, kc, vc, p['attn_sink'])
    lqd, lkd, lvd = gdn_qkv(lq_d, lk_d, lv_d, p['dn_conv'])
    lo_d, _ = gdn_bidirectional(lqd, lkd, lvd, la_d, lb_d, p['dn_a_log'], p['dn_dt_bias'],
                                (cstates[0], cstates[1]))
    branches = (fourier_branch(l_ft), gdn_output(lo_d, lz_d, p['dn_norm']),
                pool_branch(l_pool, p['pool_w'], p['pool_scale']), y_attn)
    x = x + g1 * merge_branches(branches, l_gate, p['w_branch'], p['w_out'])

    h2 = rms_norm(x, p['norm_ffn']) * (1 + sc2) + sh2
    if last:
        x = x + g2 * moe_ffn(h2.reshape(b * t, D_MODEL), p).reshape(b, t, D_MODEL)
    else:
        h2c = rms_norm(xc, p['norm_ffn']) * (1 + csc2) + csh2
        y = moe_ffn(jnp.concatenate([h2.reshape(b * t, D_MODEL), h2c.reshape(b * l, D_MODEL)], axis=0), p)
        x = x + g2 * y[:b * t].reshape(b, t, D_MODEL)
        xc = xc + cg2 * y[b * t:].reshape(b, l, D_MODEL)
    return x, xc


def setup_inputs(seed: int = 0) -> dict:
    key = jax.random.key(seed)
    ks = jax.random.split(key, 32)

    def nrm(k, shape, fan_in, gain=1.0):
        return jax.random.normal(k, shape, F32) * (gain * fan_in ** -0.5)

    def gain_vec(k, shape):
        return 1.0 + 0.05 * jax.random.normal(k, shape, F32)

    dt = jnp.exp(jax.random.uniform(ks[12], (DEPTH, 2, DN_HEADS), F32, np.log(1e-3), np.log(1e-1)))
    return {
        'x': jax.random.normal(ks[0], (BATCH, SEQ, D_MODEL), F32),
        'c': jax.random.normal(ks[1], (BATCH, D_MODEL), F32),
        'ctx': jax.random.normal(ks[2], (BATCH, CTX_LEN, D_MODEL), F32),
        'c_ctx': jax.random.normal(ks[3], (D_MODEL,), F32),
        'ada_w': nrm(ks[4], (DEPTH, D_MODEL, 6 * D_MODEL), D_MODEL, 0.5),
        'ada_b': 0.02 * jax.random.normal(ks[5], (DEPTH, 6 * D_MODEL), F32),
        'norm_mix': gain_vec(ks[6], (DEPTH, D_MODEL)),
        'norm_ffn': gain_vec(ks[7], (DEPTH, D_MODEL)),
        'w_in': nrm(ks[8], (DEPTH, D_MODEL, COLS_TOTAL), D_MODEL),
        'dn_conv': nrm(ks[9], (DEPTH, DN_CONV, 3 * DN_WIDTH), DN_CONV),
        'dn_a_log': jnp.log(jax.random.uniform(ks[10], (DEPTH, 2, DN_HEADS), F32, 1.0, 16.0)),
        'dn_dt_bias': dt + jnp.log(-jnp.expm1(-dt)),
        'dn_norm': gain_vec(ks[11], (DEPTH, HEAD_DIM)),
        'attn_sink': 0.5 * jax.random.normal(ks[13], (DEPTH, ATT_Q_HEADS), F32),
        'pool_w': nrm(ks[14], (DEPTH, len(POOL_WINDOWS), POOL_GROUP_DIM, POOL_GROUP_DIM), POOL_GROUP_DIM),
        'pool_scale': gain_vec(ks[15], (DEPTH, BRANCH_WIDTH)),
        'w_branch': nrm(ks[16], (DEPTH, N_BRANCHES, BRANCH_WIDTH, D_MODEL), BRANCH_WIDTH),
        'w_out': nrm(ks[17], (DEPTH, D_MODEL, D_MODEL), D_MODEL),
        'router_w': nrm(ks[18], (DEPTH, D_MODEL, N_EXPERTS), D_MODEL),
        'router_bias': 0.01 * jax.random.normal(ks[19], (DEPTH, N_EXPERTS), F32),
        'exp_gate': nrm(ks[20], (DEPTH, N_EXPERTS, D_MODEL, EXPERT_DIM), D_MODEL),
        'exp_up': nrm(ks[21], (DEPTH, N_EXPERTS, D_MODEL, EXPERT_DIM), D_MODEL),
        'exp_down': nrm(ks[22], (DEPTH, N_EXPERTS, EXPERT_DIM, D_MODEL), EXPERT_DIM),
        'shared_gate': nrm(ks[23], (DEPTH, D_MODEL, SHARED_DIM), D_MODEL),
        'shared_up': nrm(ks[24], (DEPTH, D_MODEL, SHARED_DIM), D_MODEL),
        'shared_down': nrm(ks[25], (DEPTH, SHARED_DIM, D_MODEL), SHARED_DIM),
        'norm_final': gain_vec(ks[26], (D_MODEL,)),
    }


def reference(x, c, ctx, c_ctx, ada_w, ada_b, norm_mix, norm_ffn, w_in, dn_conv, dn_a_log, dn_dt_bias,
              dn_norm, attn_sink, pool_w, pool_scale, w_branch, w_out, router_w, router_bias, exp_gate,
              exp_up, exp_down, shared_gate, shared_up, shared_down, norm_final):
    ang_row, ang_col = axial_rope_angles(x.shape[1])
    xc = ctx
    for i in range(DEPTH):
        mod = jax.nn.silu(c) @ ada_w[i] + ada_b[i]
        mod_c = jax.nn.silu(c_ctx) @ ada_w[i] + ada_b[i]
        p = {
            'norm_mix': norm_mix[i], 'norm_ffn': norm_ffn[i], 'w_in': w_in[i], 'dn_conv': dn_conv[i],
            'dn_a_log': dn_a_log[i], 'dn_dt_bias': dn_dt_bias[i], 'dn_norm': dn_norm[i],
            'attn_sink': attn_sink[i], 'pool_w': pool_w[i], 'pool_scale': pool_scale[i],
            'w_branch': w_branch[i], 'w_out': w_out[i], 'router_w': router_w[i],
            'router_bias': router_bias[i], 'exp_gate': exp_gate[i], 'exp_up': exp_up[i],
            'exp_down': exp_down[i], 'shared_gate': shared_gate[i], 'shared_up': shared_up[i],
            'shared_down': shared_down[i],
        }
        x, xc = layer_forward(x, xc, mod, mod_c, p, ang_row, ang_col, i == DEPTH - 1)
    return rms_norm(x, norm_final)
```

```python
import functools
import math

import jax
import jax.numpy as jnp
import numpy as np
from jax import lax
from jax.experimental import pallas as pl
from jax.experimental.pallas import tpu as pltpu

F32 = jnp.float32
BF16 = jnp.bfloat16

HEAD_DIM = 128
GRID_W = 64
ROPE_BASE = 10000.0
EPS = 1e-6
NEG_INF = -1e30
N_BRANCHES = 4
BRANCH_WIDTH = 1024
ATT_Q_HEADS = 8
ATT_KV_HEADS = 2
ATT_GROUP = ATT_Q_HEADS // ATT_KV_HEADS
WINDOW = 128
ATT_BLOCK = 128
DN_HEADS = 8
DN_WIDTH = DN_HEADS * HEAD_DIM
DN_CONV = 5
DN_CHUNK = 64
DN_SUPER = 256
FT_GROUPS = 4
FT_GROUP_DIM = BRANCH_WIDTH // FT_GROUPS
POOL_WINDOWS = (2, 4, 8, 16)
POOL_GROUP_DIM = BRANCH_WIDTH // len(POOL_WINDOWS)
N_EXPERTS = 64
N_EXPERT_GROUPS = 8
TOPK_GROUPS = 4
TOP_K = 8
EXPERT_DIM = 256
ROUTE_SCALE = 2.5

VMEM_LIMIT_BYTES = 56 * 1024 * 1024
LANES = 128

ATT_KV_W = ATT_KV_HEADS * HEAD_DIM
ATT_Q_W = ATT_Q_HEADS * HEAD_DIM
GATE_W = 16384
COL_GATE = 0
COL_QA = COL_GATE + GATE_W
COL_QD = COL_QA + ATT_Q_W
COL_KD = COL_QD + DN_WIDTH
COL_VD = COL_KD + DN_WIDTH
COL_ZD = COL_VD + DN_WIDTH
COL_FT = COL_ZD + DN_WIDTH
COL_POOL = COL_FT + BRANCH_WIDTH
COL_KA = COL_POOL + BRANCH_WIDTH
COL_VA = COL_KA + ATT_KV_W
COL_AB = COL_VA + ATT_KV_W
COLS_PAD = 24576


def _params(*sem):
    return pltpu.CompilerParams(dimension_semantics=sem, vmem_limit_bytes=VMEM_LIMIT_BYTES)


def _silu(x):
    return x * jax.nn.sigmoid(x)


def _ada_kernel(c_ref, w_ref, b_ref, o_ref):
    s = _silu(c_ref[...]).astype(BF16)
    o_ref[...] = jnp.dot(s, w_ref[...].astype(BF16), preferred_element_type=F32) + b_ref[...]


def ada_modulation(cond, ada_w, ada_b, tn=512):
    depth, d, n = ada_w.shape
    rows = cond.shape[0]
    return pl.pallas_call(
        _ada_kernel,
        out_shape=jax.ShapeDtypeStruct((depth, rows, n), F32),
        grid=(depth, n // tn),
        in_specs=[
            pl.BlockSpec((rows, d), lambda l, j: (0, 0)),
            pl.BlockSpec((None, d, tn), lambda l, j: (l, 0, j)),
            pl.BlockSpec((None, 1, tn), lambda l, j: (l, 0, j)),
        ],
        out_specs=pl.BlockSpec((None, rows, tn), lambda l, j: (l, 0, j)),
        compiler_params=_params("parallel", "parallel"),
        name="ada_modulation",
    )(cond, ada_w, ada_b.reshape(depth, 1, n))


def _modulated_norm(x, g, sc1p, sh):
    y = x * lax.rsqrt(jnp.mean(x * x, axis=-1, keepdims=True) + EPS) * g
    return y * sc1p + sh


def _proj_kernel(x_ref, g_ref, sc_ref, sh_ref, w_ref, o_ref, h_ref):
    @pl.when(pl.program_id(1) == 0)
    def _():
        h_ref[...] = _modulated_norm(x_ref[...], g_ref[...], sc_ref[...], sh_ref[...]).astype(BF16)

    o_ref[...] = jnp.dot(h_ref[...], w_ref[...], preferred_element_type=F32).astype(o_ref.dtype)


def norm_project(x2d, norm_w, sc1p, sh, w, rows_per_mod, tm=512, tn=512, out_dtype=F32):
    m, d = x2d.shape
    n = w.shape[1]
    tm = min(tm, m)
    mod_idx = lambda i, j: ((i * tm) // rows_per_mod, 0, 0)
    return pl.pallas_call(
        _proj_kernel,
        out_shape=jax.ShapeDtypeStruct((m, n), out_dtype),
        grid=(m // tm, n // tn),
        in_specs=[
            pl.BlockSpec((tm, d), lambda i, j: (i, 0)),
            pl.BlockSpec((1, d), lambda i, j: (0, 0)),
            pl.BlockSpec((None, 1, d), mod_idx),
            pl.BlockSpec((None, 1, d), mod_idx),
            pl.BlockSpec((d, tn), lambda i, j: (0, j)),
        ],
        out_specs=pl.BlockSpec((tm, tn), lambda i, j: (i, j)),
        scratch_shapes=[pltpu.VMEM((tm, d), BF16)],
        compiler_params=_params("parallel", "arbitrary"),
        name="norm_project",
    )(x2d, norm_w.reshape(1, d), sc1p, sh, w)


def _rope(x, cos, sin_signed):
    lane = lax.broadcasted_iota(jnp.int32, x.shape, 1)
    swapped = jnp.where((lane % 64) < 32, pltpu.roll(x, 96, 1), pltpu.roll(x, 32, 1))
    return x * cos + swapped * sin_signed


def _attn_kernel(*refs, banded, n_ctx, seq_len):
    if banded:
        (sink_ref, q_ref, kp_ref, kc_ref, kn_ref, vp_ref, vc_ref, vn_ref, kx_ref, vx_ref,
         cos_ref, sin_ref, o_ref) = refs
    else:
        sink_ref, q_ref, kx_ref, vx_ref, o_ref = refs
    blk = q_ref.shape[0]
    scale = HEAD_DIM ** -0.5
    n = pl.program_id(1)
    if banded:
        def table(ref, blk_idx):
            start = pl.multiple_of(blk_idx * blk, blk)
            return ref[pl.ds(start, blk), :]
        nblocks = seq_len // blk
        ip = jnp.maximum(n - 1, 0)
        inx = jnp.minimum(n + 1, nblocks - 1)
        cos_q, sin_q = table(cos_ref, n), table(sin_ref, n)
        cos_p, sin_p = table(cos_ref, ip), table(sin_ref, ip)
        cos_n, sin_n = table(cos_ref, inx), table(sin_ref, inx)
        rows = lax.broadcasted_iota(jnp.int32, (ATT_GROUP * blk, 3 * blk), 0) % blk
        cols = lax.broadcasted_iota(jnp.int32, (ATT_GROUP * blk, 3 * blk), 1)
        rel = cols - blk - rows
        kpos = (n - 1) * blk + cols
        valid = (jnp.abs(rel) <= WINDOW) & (kpos >= 0) & (kpos < seq_len)
    for h in range(ATT_KV_HEADS):
        hs = slice(h * HEAD_DIM, (h + 1) * HEAD_DIM)
        qs, sinks = [], []
        for g in range(ATT_GROUP):
            hq = h * ATT_GROUP + g
            qh = q_ref[:, hq * HEAD_DIM:(hq + 1) * HEAD_DIM]
            if banded:
                qh = _rope(qh, cos_q, sin_q)
            qs.append((qh * scale).astype(BF16))
            sinks.append(jnp.full((blk, 1), sink_ref[hq], F32))
        q4 = jnp.concatenate(qs, axis=0)
        sink = jnp.concatenate(sinks, axis=0)
        kx = kx_ref[:, hs].astype(BF16)
        vx = vx_ref[:, hs].astype(BF16)
        s_ctx = lax.dot_general(q4, kx, (((1,), (1,)), ((), ())), preferred_element_type=F32)
        m = jnp.maximum(jnp.max(s_ctx, axis=-1, keepdims=True), sink)
        if banded:
            kband = jnp.concatenate([
                _rope(kp_ref[:, hs], cos_p, sin_p), _rope(kc_ref[:, hs], cos_q, sin_q),
                _rope(kn_ref[:, hs], cos_n, sin_n)], axis=0).astype(BF16)
            vband = jnp.concatenate([vp_ref[:, hs], vc_ref[:, hs], vn_ref[:, hs]], axis=0).astype(BF16)
            s_loc = lax.dot_general(q4, kband, (((1,), (1,)), ((), ())), preferred_element_type=F32)
            s_loc = jnp.where(valid, s_loc, NEG_INF)
            m = jnp.maximum(m, jnp.max(s_loc, axis=-1, keepdims=True))
        p_ctx = jnp.exp(s_ctx - m)
        denom = jnp.sum(p_ctx, axis=-1, keepdims=True) + jnp.exp(sink - m)
        o = jnp.dot(p_ctx.astype(BF16), vx, preferred_element_type=F32)
        if banded:
            p_loc = jnp.exp(s_loc - m)
            denom = denom + jnp.sum(p_loc, axis=-1, keepdims=True)
            o = o + jnp.dot(p_loc.astype(BF16), vband, preferred_element_type=F32)
        o = o / denom
        for g in range(ATT_GROUP):
            hq = h * ATT_GROUP + g
            o_ref[:, hq * HEAD_DIM:(hq + 1) * HEAD_DIM] = o[g * blk:(g + 1) * blk].astype(o_ref.dtype)


def latent_attention(p_lat, p_ctx, sink, cos_t, sin_t, batch, seq_len, n_ctx):
    blk = ATT_BLOCK
    nb = seq_len // blk
    cq = COL_QA // ATT_Q_W
    ck, cv = COL_KA // ATT_KV_W, COL_VA // ATT_KV_W
    row = lambda b, n: b * nb + n
    kspec = lambda col, shift: pl.BlockSpec(
        (blk, ATT_KV_W), lambda b, n: (row(b, jnp.clip(n + shift, 0, nb - 1)), col))
    return pl.pallas_call(
        functools.partial(_attn_kernel, banded=True, n_ctx=n_ctx, seq_len=seq_len),
        out_shape=jax.ShapeDtypeStruct((batch * seq_len, ATT_Q_W), BF16),
        grid=(batch, nb),
        in_specs=[
            pl.BlockSpec(memory_space=pltpu.SMEM),
            pl.BlockSpec((blk, ATT_Q_W), lambda b, n: (row(b, n), cq)),
            kspec(ck, -1), kspec(ck, 0), kspec(ck, 1),
            kspec(cv, -1), kspec(cv, 0), kspec(cv, 1),
            pl.BlockSpec((n_ctx, ATT_KV_W), lambda b, n: (b, ck)),
            pl.BlockSpec((n_ctx, ATT_KV_W), lambda b, n: (b, cv)),
            pl.BlockSpec((seq_len, HEAD_DIM), lambda b, n: (0, 0)),
            pl.BlockSpec((seq_len, HEAD_DIM), lambda b, n: (0, 0)),
        ],
        out_specs=pl.BlockSpec((blk, ATT_Q_W), lambda b, n: (row(b, n), 0)),
        compiler_params=_params("parallel", "arbitrary"),
        name="latent_attention",
    )(sink, p_lat, p_lat, p_lat, p_lat, p_lat, p_lat, p_lat, p_ctx, p_ctx, cos_t, sin_t)


def context_attention(p_ctx, sink, batch, n_ctx):
    cq = COL_QA // ATT_Q_W
    ck, cv = COL_KA // ATT_KV_W, COL_VA // ATT_KV_W
    return pl.pallas_call(
        functools.partial(_attn_kernel, banded=False, n_ctx=n_ctx, seq_len=n_ctx),
        out_shape=jax.ShapeDtypeStruct((batch * n_ctx, ATT_Q_W), BF16),
        grid=(batch, 1),
        in_specs=[
            pl.BlockSpec(memory_space=pltpu.SMEM),
            pl.BlockSpec((n_ctx, ATT_Q_W), lambda b, n: (b, cq)),
            pl.BlockSpec((n_ctx, ATT_KV_W), lambda b, n: (b, ck)),
            pl.BlockSpec((n_ctx, ATT_KV_W), lambda b, n: (b, cv)),
        ],
        out_specs=pl.BlockSpec((n_ctx, ATT_Q_W), lambda b, n: (b, 0)),
        compiler_params=_params("parallel", "arbitrary"),
        name="context_attention",
    )(sink, p_ctx, p_ctx, p_ctx)


def rope_tables(seq_len):
    rows = seq_len // GRID_W
    row = jnp.repeat(jnp.arange(rows, dtype=F32), GRID_W)
    col = jnp.tile(jnp.arange(GRID_W, dtype=F32), rows)
    half = HEAD_DIM // 2
    inv_freq = ROPE_BASE ** (-jnp.arange(0, half, 2, dtype=F32) / half)
    ang_r, ang_c = row[:, None] * inv_freq, col[:, None] * inv_freq
    cos_t = jnp.concatenate([jnp.cos(ang_r)] * 2 + [jnp.cos(ang_c)] * 2, axis=-1)
    sin_t = jnp.concatenate([-jnp.sin(ang_r), jnp.sin(ang_r), -jnp.sin(ang_c), jnp.sin(ang_c)], axis=-1)
    return cos_t, sin_t


def _fourier_kernel(u_ref, cc_ref, sc_ref, ct_ref, st_ref, o_ref, p_ref, q_ref, *, scale):
    @pl.when(pl.program_id(2) == 0)
    def _():
        u = u_ref[...].astype(BF16)
        p_ref[...] = jnp.dot(u, cc_ref[...], preferred_element_type=F32).astype(BF16)
        q_ref[...] = jnp.dot(u, sc_ref[...], preferred_element_type=F32).astype(BF16)

    re = (jnp.dot(ct_ref[...], p_ref[...], preferred_element_type=F32)
          - jnp.dot(st_ref[...], q_ref[...], preferred_element_type=F32))
    o_ref[...] = (re * scale).astype(o_ref.dtype)


def dft_tables(n):
    k = jnp.arange(n, dtype=jnp.int32)
    ang = ((k[:, None] * k[None, :]) % n).astype(F32) * (2.0 * math.pi / n)
    return jnp.cos(ang).astype(BF16), jnp.sin(ang).astype(BF16)


def fourier_mix(p2d, batch, seq_len, tr=512):
    tr = min(tr, seq_len)
    cc, sc = dft_tables(FT_GROUP_DIM)
    ct, st = dft_tables(seq_len)
    c0 = COL_FT // FT_GROUP_DIM
    nr = seq_len // tr
    return pl.pallas_call(
        functools.partial(_fourier_kernel, scale=1.0 / math.sqrt(seq_len * FT_GROUP_DIM)),
        out_shape=jax.ShapeDtypeStruct((batch * seq_len, BRANCH_WIDTH), BF16),
        grid=(batch, FT_GROUPS, nr),
        in_specs=[
            pl.BlockSpec((seq_len, FT_GROUP_DIM), lambda b, g, r: (b, c0 + g)),
            pl.BlockSpec((FT_GROUP_DIM, FT_GROUP_DIM), lambda b, g, r: (0, 0)),
            pl.BlockSpec((FT_GROUP_DIM, FT_GROUP_DIM), lambda b, g, r: (0, 0)),
            pl.BlockSpec((tr, seq_len), lambda b, g, r: (r, 0)),
            pl.BlockSpec((tr, seq_len), lambda b, g, r: (r, 0)),
        ],
        out_specs=pl.BlockSpec((tr, FT_GROUP_DIM), lambda b, g, r: (b * nr + r, g)),
        scratch_shapes=[pltpu.VMEM((seq_len, FT_GROUP_DIM), BF16)] * 2,
        compiler_params=_params("parallel", "parallel", "arbitrary"),
        name="fourier_mix",
    )(p2d, cc, sc, ct, st)


POOL_TILE = 256
POOL_HALO = 128


def _pool_kernel(u_ref, band_ref, w_ref, s_ref, o_ref, pad_ref, *, seq_len):
    g = pl.program_id(1)
    zeros = jnp.zeros((POOL_HALO, POOL_GROUP_DIM), BF16)
    pad_ref[0:POOL_HALO, :] = zeros
    pad_ref[POOL_HALO + seq_len:POOL_HALO + seq_len + POOL_HALO, :] = zeros
    pad_ref[POOL_HALO:POOL_HALO + seq_len, :] = u_ref[...].astype(BF16)
    half = jnp.left_shift(1, g)
    for t in range(seq_len // POOL_TILE):
        r0 = t * POOL_TILE
        win = pad_ref[r0:r0 + POOL_TILE + 2 * POOL_HALO, :]
        sums = jnp.dot(band_ref[...], win, preferred_element_type=F32)
        pos = r0 + lax.broadcasted_iota(jnp.int32, (POOL_TILE, 1), 0)
        cnt = jnp.minimum(pos + half, seq_len) - jnp.maximum(pos - half, 0)
        m = sums / cnt.astype(F32) - u_ref[r0:r0 + POOL_TILE, :]
        y = jnp.dot(m.astype(BF16), w_ref[...], preferred_element_type=F32) * s_ref[...]
        o_ref[r0:r0 + POOL_TILE, :] = y.astype(o_ref.dtype)


def pool_bands():
    r = np.arange(POOL_TILE)[:, None]
    c = np.arange(POOL_TILE + 2 * POOL_HALO)[None, :]
    rel = c - POOL_HALO - r
    bands = [((rel >= -(w // 2)) & (rel <= w // 2 - 1)) for w in POOL_WINDOWS]
    return jnp.asarray(np.stack(bands).astype(np.float32), dtype=BF16)


def pool_mix(p2d, pool_w, pool_scale, batch, seq_len):
    c0 = COL_POOL // POOL_GROUP_DIM
    ng = len(POOL_WINDOWS)
    return pl.pallas_call(
        functools.partial(_pool_kernel, seq_len=seq_len),
        out_shape=jax.ShapeDtypeStruct((batch * seq_len, BRANCH_WIDTH), BF16),
        grid=(batch, ng),
        in_specs=[
            pl.BlockSpec((seq_len, POOL_GROUP_DIM), lambda b, g: (b, c0 + g)),
            pl.BlockSpec((None, POOL_TILE, POOL_TILE + 2 * POOL_HALO), lambda b, g: (g, 0, 0)),
            pl.BlockSpec((None, POOL_GROUP_DIM, POOL_GROUP_DIM), lambda b, g: (g, 0, 0)),
            pl.BlockSpec((1, POOL_GROUP_DIM), lambda b, g: (0, g)),
        ],
        out_specs=pl.BlockSpec((seq_len, POOL_GROUP_DIM), lambda b, g: (b, g)),
        scratch_shapes=[pltpu.VMEM((seq_len + 2 * POOL_HALO, POOL_GROUP_DIM), BF16)],
        compiler_params=_params("parallel", "parallel"),
        name="pool_mix",
    )(p2d, pool_bands(), pool_w.astype(BF16), pool_scale.reshape(1, BRANCH_WIDTH))


def permute_projection_columns(w):
    o_ka, o_va, o_kd, o_vd, o_a, o_b = 0, 256, 512, 1536, 2560, 2576
    o_qa, o_qd, o_zd, o_ft, o_pool, o_gate = 2592, 3616, 4640, 5664, 6688, 7712
    seg = lambda o, n: w[..., o:o + n]
    parts = [seg(o_gate, GATE_W), seg(o_qa, ATT_Q_W), seg(o_qd, DN_WIDTH), seg(o_kd, DN_WIDTH),
             seg(o_vd, DN_WIDTH), seg(o_zd, DN_WIDTH), seg(o_ft, BRANCH_WIDTH), seg(o_pool, BRANCH_WIDTH),
             seg(o_ka, ATT_KV_W), seg(o_va, ATT_KV_W), seg(o_a, 2 * DN_HEADS), seg(o_b, 2 * DN_HEADS)]
    used = sum(p.shape[-1] for p in parts)
    parts.append(jnp.zeros(w.shape[:-1] + (COLS_PAD - used,), w.dtype))
    return jnp.concatenate(parts, axis=-1)


CONV_PAD = 8
N_LEVELS = 6


def dn_masks():
    i = np.arange(DN_SUPER)[:, None]
    j = np.arange(DN_SUPER)[None, :]
    incl, lev = [], []
    for later in (lambda a, b: a >= b, lambda a, b: a <= b):
        strict = later(i, j) & (i != j)
        incl.append((i // DN_CHUNK == j // DN_CHUNK) & later(i, j))
        lv = [(i // 2 == j // 2) & strict]
        sz = 2
        while sz < DN_CHUNK:
            lv.append((i // (2 * sz) == j // (2 * sz)) & (i // sz != j // sz) & strict)
            sz *= 2
        lev.append(np.stack(lv))
    return (jnp.asarray(np.stack(incl).astype(np.float32)), jnp.asarray(np.stack(lev).astype(np.float32)))


def _split3(x):
    x1 = x.astype(BF16)
    r1 = x - x1.astype(F32)
    x2 = r1.astype(BF16)
    x3 = (r1 - x2.astype(F32)).astype(BF16)
    return x1, x2, x3


def _softplus(x):
    return jnp.maximum(x, 0.0) + jnp.log1p(jnp.exp(-jnp.abs(x)))


def _dn_kernel(alog_ref, dtb_ref, q_ref, k_ref, v_ref, z_ref, ab_ref, cq_ref, ck_ref, cv_ref, nw_ref,
               s0_ref, incl_ref, lev_ref, o_ref, sfin_ref,
               pq_ref, pk_ref, pv_ref, u_ref, w_ref, qe_ref, kd_ref, eg_ref, oacc_ref, *, seq_len, with_q):
    h = pl.program_id(1)
    n_super = seq_len // DN_SUPER
    n_chunks = seq_len // DN_CHUNK
    per_super = DN_SUPER // DN_CHUNK
    zpad = jnp.zeros((CONV_PAD, HEAD_DIM), F32)
    streams = [(pk_ref, k_ref), (pv_ref, v_ref)] + ([(pq_ref, q_ref)] if with_q else [])
    for pad_ref, src_ref in streams:
        pad_ref[0:CONV_PAD, :] = zpad
        pad_ref[CONV_PAD + seq_len:2 * CONV_PAD + seq_len, :] = zpad
        pad_ref[CONV_PAD:CONV_PAD + seq_len, :] = src_ref[...]

    lane = lax.broadcasted_iota(jnp.int32, (1, HEAD_DIM), 1)
    pick = lambda ref: jnp.where(lane == 0, ref[0, h], jnp.where(lane == 1, ref[1, h], 0.0))
    neg_a = -jnp.exp(pick(alog_ref))
    dtb = pick(dtb_ref)
    eye = (lax.broadcasted_iota(jnp.int32, (DN_SUPER, DN_SUPER), 0)
           == lax.broadcasted_iota(jnp.int32, (DN_SUPER, DN_SUPER), 1)).astype(F32)
    half = DN_CONV // 2

    def conv_silu(pad_ref, cw_ref, r0):
        win = pad_ref[pl.ds(r0, DN_SUPER + 2 * CONV_PAD), :]
        acc = None
        for j in range(DN_CONV):
            off = CONV_PAD - half + j
            term = cw_ref[j:j + 1, :] * win[off:off + DN_SUPER, :]
            acc = term if acc is None else acc + term
        return _silu(acc)

    def l2n(x):
        return x * lax.rsqrt(jnp.sum(x * x, axis=-1, keepdims=True) + EPS)

    def super_chunk(sidx, carry):
        r0 = pl.multiple_of(sidx * DN_SUPER, DN_SUPER)
        k = l2n(conv_silu(pk_ref, ck_ref, r0))
        v = conv_silu(pv_ref, cv_ref, r0)
        kb = k.astype(BF16)
        kk = lax.dot_general(kb, kb, (((1,), (1,)), ((), ())), preferred_element_type=F32)
        if with_q:
            q = l2n(conv_silu(pq_ref, cq_ref, r0)) * (HEAD_DIM ** -0.5)
            qk = lax.dot_general(q.astype(BF16), kb, (((1,), (1,)), ((), ())), preferred_element_type=F32)
        ab = ab_ref[pl.ds(r0, DN_SUPER), :]
        gt = neg_a * _softplus(ab + dtb)
        bt = jax.nn.sigmoid(ab)
        gparts = _split3(gt)
        gcs = []
        for d in range(2):
            seg = incl_ref[d].astype(BF16)
            gcs.append(sum(jnp.dot(seg, gp, preferred_element_type=F32) for gp in gparts))
        gtot_t = gcs[0] + gcs[1] - gt
        for d in range(2):
            incl = incl_ref[d]
            gcol = gcs[d][:, d:d + 1]
            grow = jnp.transpose(gcs[d])[d:d + 1, :]
            gtot = gtot_t[:, d:d + 1]
            beta = bt[:, 2 + d:3 + d]
            e = jnp.exp((gcol - grow) * incl)
            nf = (beta * kk) * e
            t = eye - nf * lev_ref[d, 0]
            for m in range(1, N_LEVELS):
                tb = t.astype(BF16)
                x = jnp.dot((nf * lev_ref[d, m]).astype(BF16), tb, preferred_element_type=F32)
                t = t - jnp.dot(tb, x.astype(BF16), preferred_element_type=F32)
            egc = jnp.exp(gcol)
            rhs = jnp.concatenate([beta * v, (beta * egc) * k], axis=1).astype(BF16)
            sol = jnp.dot(t.astype(BF16), rhs, preferred_element_type=F32)
            u, w = sol[:, :HEAD_DIM], sol[:, HEAD_DIM:]
            rows = pl.ds(r0, DN_SUPER)
            u_ref[d, rows, :] = u
            w_ref[d, rows, :] = w.astype(BF16)
            kd_ref[d, rows, :] = (k * jnp.exp(gtot - gcol)).astype(BF16)
            if with_q:
                qkm = (qk * e * incl).astype(BF16)
                r2 = jnp.dot(qkm, jnp.concatenate([w, u], axis=1).astype(BF16), preferred_element_type=F32)
                qe_ref[d, rows, :] = (q * egc - r2[:, :HEAD_DIM]).astype(BF16)
                oacc_ref[d, rows, :] = r2[:, HEAD_DIM:]
            eg = jnp.exp(gtot)
            for c in range(per_super):
                blk = jnp.broadcast_to(eg[c * DN_CHUNK:c * DN_CHUNK + 8, :], (8, HEAD_DIM))
                eg_ref[d, pl.ds(pl.multiple_of((sidx * per_super + c) * 8, 8), 8), :] = blk
        return carry

    lax.fori_loop(0, n_super, super_chunk, 0)

    def chunk_step(c, states):
        new_states = []
        for d in range(2):
            s = states[d]
            cc = c if d == 0 else n_chunks - 1 - c
            rows = pl.ds(pl.multiple_of(cc * DN_CHUNK, DN_CHUNK), DN_CHUNK)
            sb = s.astype(BF16)
            corr = jnp.dot(w_ref[d, rows, :], sb, preferred_element_type=F32)
            vnew = u_ref[d, rows, :] - corr
            if with_q:
                oacc_ref[d, rows, :] += jnp.dot(qe_ref[d, rows, :], sb, preferred_element_type=F32)
            decay = eg_ref[d, pl.ds(pl.multiple_of(cc * 8, 8), 8), :][0:1, :]
            upd = lax.dot_general(kd_ref[d, rows, :], vnew.astype(BF16), (((0,), (0,)), ((), ())),
                                  preferred_element_type=F32)
            new_states.append(decay * s + upd)
        return tuple(new_states)

    s_f, s_b = lax.fori_loop(0, n_chunks, chunk_step, (s0_ref[0], s0_ref[1]))
    sfin_ref[0] = s_f
    sfin_ref[1] = s_b
    if with_q:
        o = oacc_ref[0] + oacc_ref[1]
        on = o * lax.rsqrt(jnp.mean(o * o, axis=-1, keepdims=True) + EPS) * nw_ref[...]
        o_ref[...] = (on * _silu(z_ref[...])).astype(o_ref.dtype)
    else:
        o_ref[...] = jnp.zeros(o_ref.shape, o_ref.dtype)


def head_gate_layout(p2d):
    m = p2d.shape[0]
    ab = p2d[:, COL_AB:COL_AB + 4 * DN_HEADS].astype(F32).reshape(m, 4, DN_HEADS)
    ab = jnp.transpose(ab, (0, 2, 1))
    ab = jnp.pad(ab, ((0, 0), (0, 0), (0, LANES - 4)))
    return ab.reshape(m, DN_HEADS * LANES)


def deltanet_mix(p2d, a_log, dt_bias, conv_w, norm_w, s0, batch, seq_len, with_q=True):
    incl, lev = dn_masks()
    ab = head_gate_layout(p2d)
    cqb, ckb, cvb, czb = (c // HEAD_DIM for c in (COL_QD, COL_KD, COL_VD, COL_ZD))
    seq_spec = lambda c0: pl.BlockSpec((seq_len, HEAD_DIM), lambda b, h: (b, c0 + h))
    conv_spec = lambda c0: pl.BlockSpec((DN_CONV, HEAD_DIM), lambda b, h: (0, c0 + h))
    state_spec = pl.BlockSpec((None, None, 2, HEAD_DIM, HEAD_DIM), lambda b, h: (b, h, 0, 0, 0))
    smem = pl.BlockSpec(memory_space=pltpu.SMEM)
    n_chunks = seq_len // DN_CHUNK
    per_dir = lambda dt: pltpu.VMEM((2, seq_len, HEAD_DIM), dt)
    pad = pltpu.VMEM((seq_len + 2 * CONV_PAD, HEAD_DIM), F32)
    return pl.pallas_call(
        functools.partial(_dn_kernel, seq_len=seq_len, with_q=with_q),
        out_shape=(jax.ShapeDtypeStruct((batch * seq_len, DN_WIDTH), BF16),
                   jax.ShapeDtypeStruct((batch, DN_HEADS, 2, HEAD_DIM, HEAD_DIM), F32)),
        grid=(batch, DN_HEADS),
        in_specs=[
            smem, smem,
            seq_spec(cqb), seq_spec(ckb), seq_spec(cvb), seq_spec(czb),
            pl.BlockSpec((seq_len, LANES), lambda b, h: (b, h)),
            conv_spec(0), conv_spec(DN_HEADS), conv_spec(2 * DN_HEADS),
            pl.BlockSpec((1, HEAD_DIM), lambda b, h: (0, 0)),
            state_spec,
            pl.BlockSpec((2, DN_SUPER, DN_SUPER), lambda b, h: (0, 0, 0)),
            pl.BlockSpec((2, N_LEVELS, DN_SUPER, DN_SUPER), lambda b, h: (0, 0, 0, 0)),
        ],
        out_specs=(pl.BlockSpec((seq_len, HEAD_DIM), lambda b, h: (b, h)), state_spec),
        scratch_shapes=[pad, pad, pad, per_dir(F32), per_dir(BF16), per_dir(BF16), per_dir(BF16),
                        pltpu.VMEM((2, n_chunks * 8, HEAD_DIM), F32), per_dir(F32)],
        compiler_params=_params("parallel", "parallel"),
        name="deltanet_mix",
    )(a_log, dt_bias, p2d, p2d, p2d, p2d, ab, conv_w, conv_w, conv_w, norm_w.reshape(1, HEAD_DIM),
      s0, incl, lev)


def _merge_gate_kernel(y0, y1, y2, y3, g0, g1, g2, g3, wb_ref, o_ref):
    acc = None
    for n, (y_ref, g_ref) in enumerate(((y0, g0), (y1, g1), (y2, g2), (y3, g3))):
        t = jnp.dot(y_ref[...], wb_ref[n], preferred_element_type=F32) * jax.nn.sigmoid(g_ref[...].astype(F32))
        acc = t if acc is None else acc + t
    o_ref[...] = acc.astype(o_ref.dtype)


def merge_gate(branches, p2d, wb, tm=512, tn=1024):
    m = p2d.shape[0]
    d = wb.shape[-1]
    tm = min(tm, m)
    nj = d // tn
    y_spec = pl.BlockSpec((tm, BRANCH_WIDTH), lambda i, j: (i, 0))
    g_spec = lambda n: pl.BlockSpec((tm, tn), lambda i, j: (i, COL_GATE // tn + n * nj + j))
    return pl.pallas_call(
        _merge_gate_kernel,
        out_shape=jax.ShapeDtypeStruct((m, d), BF16),
        grid=(m // tm, nj),
        in_specs=[y_spec] * N_BRANCHES + [g_spec(n) for n in range(N_BRANCHES)]
        + [pl.BlockSpec((N_BRANCHES, BRANCH_WIDTH, tn), lambda i, j: (0, 0, j))],
        out_specs=pl.BlockSpec((tm, tn), lambda i, j: (i, j)),
        compiler_params=_params("parallel", "arbitrary"),
        name="merge_gate",
    )(*branches, p2d, p2d, p2d, p2d, wb)


def _out_proj_kernel(a_ref, w_ref, x_ref, g_ref, o_ref):
    o_ref[...] = x_ref[...] + g_ref[...] * jnp.dot(a_ref[...], w_ref[...], preferred_element_type=F32)


def out_project_residual(a, w, x2d, gate, rows_per_mod, tm=512, tn=1024):
    m, d = x2d.shape
    tm = min(tm, m)
    return pl.pallas_call(
        _out_proj_kernel,
        out_shape=jax.ShapeDtypeStruct((m, d), F32),
        grid=(m // tm, d // tn),
        in_specs=[
            pl.BlockSpec((tm, a.shape[1]), lambda i, j: (i, 0)),
            pl.BlockSpec((a.shape[1], tn), lambda i, j: (0, j)),
            pl.BlockSpec((tm, tn), lambda i, j: (i, j)),
            pl.BlockSpec((None, 1, tn), lambda i, j: ((i * tm) // rows_per_mod, 0, j)),
        ],
        out_specs=pl.BlockSpec((tm, tn), lambda i, j: (i, j)),
        compiler_params=_params("parallel", "arbitrary"),
        name="out_project_residual",
    )(a, w, x2d, gate)


def _first_argmax(v, lane):
    m = jnp.max(v, axis=-1, keepdims=True)
    idx = jnp.min(jnp.where(v == m, lane, LANES), axis=-1, keepdims=True)
    return m, idx


def _route_kernel(x_ref, g_ref, sc_ref, sh_ref, rw_ref, rb_ref, h_ref, gate_ref):
    h = _modulated_norm(x_ref[...], g_ref[...], sc_ref[...], sh_ref[...])
    h_ref[...] = h.astype(BF16)
    logits = jnp.dot(h, rw_ref[...], preferred_element_type=F32, precision=lax.Precision.HIGHEST)
    scores = jax.nn.sigmoid(logits)
    lane = lax.broadcasted_iota(jnp.int32, scores.shape, 1)
    grp = lane // (N_EXPERTS // N_EXPERT_GROUPS)
    ninf = -jnp.inf
    vb = jnp.where(lane < N_EXPERTS, scores + rb_ref[...], ninf)
    gs = jnp.full(scores.shape, ninf, F32)
    for g in range(N_EXPERT_GROUPS):
        vg = jnp.where(grp == g, vb, ninf)
        m1, i1 = _first_argmax(vg, lane)
        m2 = jnp.max(jnp.where(lane == i1, ninf, vg), axis=-1, keepdims=True)
        gs = jnp.where(lane == g, m1 + m2, gs)
    allowed = jnp.zeros(scores.shape, F32)
    for _ in range(TOPK_GROUPS):
        _, gi = _first_argmax(gs, lane)
        gs = jnp.where(lane == gi, ninf, gs)
        allowed = jnp.where(grp == gi, 1.0, allowed)
    ve = jnp.where(allowed > 0, vb, ninf)
    sel = jnp.zeros(scores.shape, F32)
    for _ in range(TOP_K):
        _, ei = _first_argmax(ve, lane)
        hit = lane == ei
        sel = jnp.where(hit, 1.0, sel)
        ve = jnp.where(hit, ninf, ve)
    w = sel * scores
    w = w / jnp.sum(w, axis=-1, keepdims=True) * ROUTE_SCALE
    gate_ref[...] = jnp.where((lane >= N_EXPERTS) & (lane < N_EXPERTS + 2), 1.0, w)


def route(x2d, norm_w, sc1p, sh, router_w, router_bias, rows_per_mod, tm=256):
    m, d = x2d.shape
    tm = min(tm, m)
    rw = jnp.pad(router_w, ((0, 0), (0, LANES - N_EXPERTS)))
    rb = jnp.pad(router_bias, (0, LANES - N_EXPERTS)).reshape(1, LANES)
    mod_idx = lambda i: ((i * tm) // rows_per_mod, 0, 0)
    return pl.pallas_call(
        _route_kernel,
        out_shape=(jax.ShapeDtypeStruct((m, d), BF16), jax.ShapeDtypeStruct((m, LANES), F32)),
        grid=(m // tm,),
        in_specs=[
            pl.BlockSpec((tm, d), lambda i: (i, 0)),
            pl.BlockSpec((1, d), lambda i: (0, 0)),
            pl.BlockSpec((None, 1, d), mod_idx),
            pl.BlockSpec((None, 1, d), mod_idx),
            pl.BlockSpec((d, LANES), lambda i: (0, 0)),
            pl.BlockSpec((1, LANES), lambda i: (0, 0)),
        ],
        out_specs=(pl.BlockSpec((tm, d), lambda i: (i, 0)), pl.BlockSpec((tm, LANES), lambda i: (i, 0))),
        compiler_params=_params("parallel"),
        name="route",
    )(x2d, norm_w.reshape(1, d), sc1p, sh, rw, rb)


def _ffn_kernel(h_ref, gate_ref, wg_ref, wu_ref, wd_ref, o_ref, acc_ref):
    e = pl.program_id(1)

    @pl.when(e == 0)
    def _():
        acc_ref[...] = jnp.zeros(acc_ref.shape, F32)

    h = h_ref[...]
    a = jnp.dot(h, wg_ref[...], preferred_element_type=F32)
    u = jnp.dot(h, wu_ref[...], preferred_element_type=F32)
    lane = lax.broadcasted_iota(jnp.int32, gate_ref.shape, 1)
    gcol = jnp.sum(jnp.where(lane == e, gate_ref[...], 0.0), axis=-1, keepdims=True)
    hm = (_silu(a) * u * gcol).astype(BF16)
    acc_ref[...] += jnp.dot(hm, wd_ref[...], preferred_element_type=F32)

    @pl.when(e == pl.num_programs(1) - 1)
    def _():
        o_ref[...] = acc_ref[...].astype(o_ref.dtype)


def dense_experts(h, gates, wg, wu, wd, tm=512):
    m, d = h.shape
    ne, _, f = wg.shape
    tm = min(tm, m)
    return pl.pallas_call(
        _ffn_kernel,
        out_shape=jax.ShapeDtypeStruct((m, d), BF16),
        grid=(m // tm, ne),
        in_specs=[
            pl.BlockSpec((tm, d), lambda i, e: (i, 0)),
            pl.BlockSpec((tm, LANES), lambda i, e: (i, 0)),
            pl.BlockSpec((None, d, f), lambda i, e: (e, 0, 0)),
            pl.BlockSpec((None, d, f), lambda i, e: (e, 0, 0)),
            pl.BlockSpec((None, f, d), lambda i, e: (e, 0, 0)),
        ],
        out_specs=pl.BlockSpec((tm, d), lambda i, e: (i, 0)),
        scratch_shapes=[pltpu.VMEM((tm, d), F32)],
        compiler_params=_params("parallel", "arbitrary"),
        name="dense_experts",
    )(h, gates, wg, wu, wd)


def _residual_kernel(x_ref, y_ref, g_ref, nf_ref, o_ref, *, final_norm):
    x = x_ref[...] + g_ref[...] * y_ref[...].astype(F32)
    if final_norm:
        x = x * lax.rsqrt(jnp.mean(x * x, axis=-1, keepdims=True) + EPS) * nf_ref[...]
    o_ref[...] = x


def gated_residual(x2d, y, gate, norm_final, rows_per_mod, final_norm, tm=256):
    m, d = x2d.shape
    tm = min(tm, m)
    return pl.pallas_call(
        functools.partial(_residual_kernel, final_norm=final_norm),
        out_shape=jax.ShapeDtypeStruct((m, d), F32),
        grid=(m // tm,),
        in_specs=[
            pl.BlockSpec((tm, d), lambda i: (i, 0)),
            pl.BlockSpec((tm, d), lambda i: (i, 0)),
            pl.BlockSpec((None, 1, d), lambda i: ((i * tm) // rows_per_mod, 0, 0)),
            pl.BlockSpec((1, d), lambda i: (0, 0)),
        ],
        out_specs=pl.BlockSpec((tm, d), lambda i: (i, 0)),
        compiler_params=_params("parallel"),
        name="gated_residual",
    )(x2d, y, gate, norm_final.reshape(1, d))


def kernel(x, c, ctx, c_ctx, ada_w, ada_b, norm_mix, norm_ffn, w_in, dn_conv, dn_a_log, dn_dt_bias, dn_norm, attn_sink, pool_w, pool_scale, w_branch, w_out, router_w, router_bias, exp_gate, exp_up, exp_down, shared_gate, shared_up, shared_down, norm_final):
    batch, seq_len, d = x.shape
    n_ctx = ctx.shape[1]
    depth = ada_w.shape[0]
    cond_rows = 16
    cond = jnp.zeros((cond_rows, d), F32).at[:batch].set(c).at[batch].set(c_ctx)
    mod_all = ada_modulation(cond, ada_w, ada_b).reshape(depth, cond_rows, 6, 1, d)
    cos_t, sin_t = rope_tables(seq_len)
    xl = x.reshape(batch * seq_len, d)
    xc = ctx.reshape(batch * n_ctx, d)
    zero_state = jnp.zeros((batch, DN_HEADS, 2, HEAD_DIM, HEAD_DIM), F32)
    for i in range(depth):
        last = i == depth - 1
        sh1, sc1, g1, sh2, sc2, g2 = (mod_all[i, :, s] for s in range(6))
        lat = lambda m: m[:batch]
        cx = lambda m: m[batch:batch + 1]
        w_perm = permute_projection_columns(w_in[i]).astype(BF16)
        wb = w_branch[i].astype(BF16)
        wo = w_out[i].astype(BF16)
        p_ctx = norm_project(xc, norm_mix[i], 1.0 + cx(sc1), cx(sh1), w_perm, batch * n_ctx)
        p_lat = norm_project(xl, norm_mix[i], 1.0 + lat(sc1), lat(sh1), w_perm, seq_len)

        y_dn_c, s_ctx = deltanet_mix(p_ctx, dn_a_log[i], dn_dt_bias[i], dn_conv[i], dn_norm[i], zero_state,
                                     batch, n_ctx, with_q=not last)
        if not last:
            branches_c = (fourier_mix(p_ctx, batch, n_ctx), y_dn_c,
                          pool_mix(p_ctx, pool_w[i], pool_scale[i], batch, n_ctx),
                          context_attention(p_ctx, attn_sink[i], batch, n_ctx))
            xc = out_project_residual(merge_gate(branches_c, p_ctx, wb), wo, xc, cx(g1), batch * n_ctx)

        y_dn, _ = deltanet_mix(p_lat, dn_a_log[i], dn_dt_bias[i], dn_conv[i], dn_norm[i], s_ctx, batch, seq_len)
        branches = (fourier_mix(p_lat, batch, seq_len), y_dn,
                    pool_mix(p_lat, pool_w[i], pool_scale[i], batch, seq_len),
                    latent_attention(p_lat, p_ctx, attn_sink[i], cos_t, sin_t, batch, seq_len, n_ctx))
        xl = out_project_residual(merge_gate(branches, p_lat, wb), wo, xl, lat(g1), seq_len)

        f = EXPERT_DIM
        stack_cols = lambda s: jnp.transpose(s.reshape(d, -1, f), (1, 0, 2))
        wg = jnp.concatenate([exp_gate[i], stack_cols(shared_gate[i])], axis=0).astype(BF16)
        wu = jnp.concatenate([exp_up[i], stack_cols(shared_up[i])], axis=0).astype(BF16)
        wd = jnp.concatenate([exp_down[i], shared_down[i].reshape(-1, f, d)], axis=0).astype(BF16)
        h2, gates = route(xl, norm_ffn[i], 1.0 + lat(sc2), lat(sh2), router_w[i], router_bias[i], seq_len)
        xl = gated_residual(xl, dense_experts(h2, gates, wg, wu, wd), lat(g2), norm_final, seq_len, last)
        if not last:
            h2c, gates_c = route(xc, norm_ffn[i], 1.0 + cx(sc2), cx(sh2), router_w[i], router_bias[i],
                                 batch * n_ctx)
            xc = gated_residual(xc, dense_experts(h2c, gates_c, wg, wu, wd), cx(g2), norm_final,
                                batch * n_ctx, False)
    return xl.reshape(batch, seq_len, d)
```

```python
import functools
import math

import jax
import jax.numpy as jnp
import numpy as np
from jax import lax
from jax.experimental import pallas as pl
from jax.experimental.pallas import tpu as pltpu

F32 = jnp.float32
BF16 = jnp.bfloat16

HEAD_DIM = 128
GRID_W = 64
ROPE_BASE = 10000.0
EPS = 1e-6
NEG_INF = -1e30
N_BRANCHES = 4
BRANCH_WIDTH = 1024
ATT_Q_HEADS = 8
ATT_KV_HEADS = 2
ATT_GROUP = ATT_Q_HEADS // ATT_KV_HEADS
WINDOW = 128
ATT_BLOCK = 128
DN_HEADS = 8
DN_WIDTH = DN_HEADS * HEAD_DIM
DN_CONV = 5
DN_CHUNK = 64
DN_SUPER = 256
FT_GROUPS = 4
FT_GROUP_DIM = BRANCH_WIDTH // FT_GROUPS
POOL_WINDOWS = (2, 4, 8, 16)
POOL_GROUP_DIM = BRANCH_WIDTH // len(POOL_WINDOWS)
N_EXPERTS = 64
N_EXPERT_GROUPS = 8
TOPK_GROUPS = 4
TOP_K = 8
EXPERT_DIM = 256
ROUTE_SCALE = 2.5

VMEM_LIMIT_BYTES = 56 * 1024 * 1024
LANES = 128

ATT_KV_W = ATT_KV_HEADS * HEAD_DIM
ATT_Q_W = ATT_Q_HEADS * HEAD_DIM
GATE_W = 16384
COL_GATE = 0
COL_QA = COL_GATE + GATE_W
COL_QD = COL_QA + ATT_Q_W
COL_KD = COL_QD + DN_WIDTH
COL_VD = COL_KD + DN_WIDTH
COL_ZD = COL_VD + DN_WIDTH
COL_FT = COL_ZD + DN_WIDTH
COL_POOL = COL_FT + BRANCH_WIDTH
COL_KA = COL_POOL + BRANCH_WIDTH
COL_VA = COL_KA + ATT_KV_W
COL_AB = COL_VA + ATT_KV_W
COLS_PAD = 24576


def _params(*sem):
    return pltpu.CompilerParams(dimension_semantics=sem, vmem_limit_bytes=VMEM_LIMIT_BYTES)


def _silu(x):
    return x * jax.nn.sigmoid(x)


def _ada_kernel(c_ref, w_ref, b_ref, o_ref):
    s = _silu(c_ref[...]).astype(BF16)
    o_ref[...] = jnp.dot(s, w_ref[...].astype(BF16), preferred_element_type=F32) + b_ref[...]


def ada_modulation(cond, ada_w, ada_b, tn=512):
    depth, d, n = ada_w.shape
    rows = cond.shape[0]
    return pl.pallas_call(
        _ada_kernel,
        out_shape=jax.ShapeDtypeStruct((depth, rows, n), F32),
        grid=(depth, n // tn),
        in_specs=[
            pl.BlockSpec((rows, d), lambda l, j: (0, 0)),
            pl.BlockSpec((None, d, tn), lambda l, j: (l, 0, j)),
            pl.BlockSpec((None, 1, tn), lambda l, j: (l, 0, j)),
        ],
        out_specs=pl.BlockSpec((None, rows, tn), lambda l, j: (l, 0, j)),
        compiler_params=_params("parallel", "parallel"),
        name="ada_modulation",
    )(cond, ada_w, ada_b.reshape(depth, 1, n))


def _modulated_norm(x, g, sc1p, sh):
    y = x * lax.rsqrt(jnp.mean(x * x, axis=-1, keepdims=True) + EPS) * g
    return y * sc1p + sh


def _norm_kernel(x_ref, g_ref, sc_ref, sh_ref, o_ref):
    o_ref[...] = _modulated_norm(x_ref[...], g_ref[...], sc_ref[...], sh_ref[...]).astype(o_ref.dtype)


def _matmul_kernel(a_ref, w_ref, o_ref):
    o_ref[...] = jnp.dot(a_ref[...], w_ref[...], preferred_element_type=F32).astype(o_ref.dtype)


def norm_project(x2d, norm_w, sc1p, sh, w, rows_per_mod, tm=1024, tn=1024, out_dtype=F32):
    m, d = x2d.shape
    n = w.shape[1]
    tr = min(256, m)
    mod_idx = lambda i: ((i * tr) // rows_per_mod, 0, 0)
    hn = pl.pallas_call(
        _norm_kernel,
        out_shape=jax.ShapeDtypeStruct((m, d), BF16),
        grid=(m // tr,),
        in_specs=[
            pl.BlockSpec((tr, d), lambda i: (i, 0)),
            pl.BlockSpec((1, d), lambda i: (0, 0)),
            pl.BlockSpec((None, 1, d), mod_idx),
            pl.BlockSpec((None, 1, d), mod_idx),
        ],
        out_specs=pl.BlockSpec((tr, d), lambda i: (i, 0)),
        compiler_params=_params("parallel"),
        name="modulated_norm",
    )(x2d, norm_w.reshape(1, d), sc1p, sh)
    tm = min(tm, m)
    return pl.pallas_call(
        _matmul_kernel,
        out_shape=jax.ShapeDtypeStruct((m, n), out_dtype),
        grid=(m // tm, n // tn),
        in_specs=[
            pl.BlockSpec((tm, d), lambda i, j: (i, 0)),
            pl.BlockSpec((d, tn), lambda i, j: (0, j)),
        ],
        out_specs=pl.BlockSpec((tm, tn), lambda i, j: (i, j)),
        compiler_params=_params("parallel", "arbitrary"),
        name="project",
    )(hn, w)


def _rope(x, cos, sin_signed):
    lane = lax.broadcasted_iota(jnp.int32, x.shape, 1)
    swapped = jnp.where((lane % 64) < 32, pltpu.roll(x, 96, 1), pltpu.roll(x, 32, 1))
    return x * cos + swapped * sin_signed


def _attn_kernel(*refs, banded, n_ctx, seq_len):
    if banded:
        (sink_ref, q_ref, kp_ref, kc_ref, kn_ref, vp_ref, vc_ref, vn_ref, kx_ref, vx_ref,
         cos_ref, sin_ref, o_ref) = refs
    else:
        sink_ref, q_ref, kx_ref, vx_ref, o_ref = refs
    blk = q_ref.shape[0]
    scale = HEAD_DIM ** -0.5
    n = pl.program_id(1)
    if banded:
        def table(ref, blk_idx):
            start = pl.multiple_of(blk_idx * blk, blk)
            return ref[pl.ds(start, blk), :]
        nblocks = seq_len // blk
        ip = jnp.maximum(n - 1, 0)
        inx = jnp.minimum(n + 1, nblocks - 1)
        cos_q, sin_q = table(cos_ref, n), table(sin_ref, n)
        cos_p, sin_p = table(cos_ref, ip), table(sin_ref, ip)
        cos_n, sin_n = table(cos_ref, inx), table(sin_ref, inx)
        rows = lax.broadcasted_iota(jnp.int32, (ATT_GROUP * blk, 3 * blk), 0) % blk
        cols = lax.broadcasted_iota(jnp.int32, (ATT_GROUP * blk, 3 * blk), 1)
        rel = cols - blk - rows
        kpos = (n - 1) * blk + cols
        valid = (jnp.abs(rel) <= WINDOW) & (kpos >= 0) & (kpos < seq_len)
    for h in range(ATT_KV_HEADS):
        hs = slice(h * HEAD_DIM, (h + 1) * HEAD_DIM)
        qs, sinks = [], []
        for g in range(ATT_GROUP):
            hq = h * ATT_GROUP + g
            qh = q_ref[:, hq * HEAD_DIM:(hq + 1) * HEAD_DIM]
            if banded:
                qh = _rope(qh, cos_q, sin_q)
            qs.append((qh * scale).astype(BF16))
            sinks.append(jnp.full((blk, 1), sink_ref[hq], F32))
        q4 = jnp.concatenate(qs, axis=0)
        sink = jnp.concatenate(sinks, axis=0)
        kx = kx_ref[:, hs].astype(BF16)
        vx = vx_ref[:, hs].astype(BF16)
        s_ctx = lax.dot_general(q4, kx, (((1,), (1,)), ((), ())), preferred_element_type=F32)
        m = jnp.maximum(jnp.max(s_ctx, axis=-1, keepdims=True), sink)
        if banded:
            kband = jnp.concatenate([
                _rope(kp_ref[:, hs], cos_p, sin_p), _rope(kc_ref[:, hs], cos_q, sin_q),
                _rope(kn_ref[:, hs], cos_n, sin_n)], axis=0).astype(BF16)
            vband = jnp.concatenate([vp_ref[:, hs], vc_ref[:, hs], vn_ref[:, hs]], axis=0).astype(BF16)
            s_loc = lax.dot_general(q4, kband, (((1,), (1,)), ((), ())), preferred_element_type=F32)
            s_loc = jnp.where(valid, s_loc, NEG_INF)
            m = jnp.maximum(m, jnp.max(s_loc, axis=-1, keepdims=True))
        p_ctx = jnp.exp(s_ctx - m)
        denom = jnp.sum(p_ctx, axis=-1, keepdims=True) + jnp.exp(sink - m)
        o = jnp.dot(p_ctx.astype(BF16), vx, preferred_element_type=F32)
        if banded:
            p_loc = jnp.exp(s_loc - m)
            denom = denom + jnp.sum(p_loc, axis=-1, keepdims=True)
            o = o + jnp.dot(p_loc.astype(BF16), vband, preferred_element_type=F32)
        o = o / denom
        for g in range(ATT_GROUP):
            hq = h * ATT_GROUP + g
            o_ref[:, hq * HEAD_DIM:(hq + 1) * HEAD_DIM] = o[g * blk:(g + 1) * blk].astype(o_ref.dtype)


def latent_attention(p_lat, p_ctx, sink, cos_t, sin_t, batch, seq_len, n_ctx, ctx_col0=0):
    blk = ATT_BLOCK
    nb = seq_len // blk
    cq = COL_QA // ATT_Q_W
    ck, cv = COL_KA // ATT_KV_W, COL_VA // ATT_KV_W
    xk, xv = (COL_KA - ctx_col0) // ATT_KV_W, (COL_VA - ctx_col0) // ATT_KV_W
    row = lambda b, n: b * nb + n
    kspec = lambda col, shift: pl.BlockSpec(
        (blk, ATT_KV_W), lambda b, n: (row(b, jnp.clip(n + shift, 0, nb - 1)), col))
    return pl.pallas_call(
        functools.partial(_attn_kernel, banded=True, n_ctx=n_ctx, seq_len=seq_len),
        out_shape=jax.ShapeDtypeStruct((batch * seq_len, ATT_Q_W), BF16),
        grid=(batch, nb),
        in_specs=[
            pl.BlockSpec(memory_space=pltpu.SMEM),
            pl.BlockSpec((blk, ATT_Q_W), lambda b, n: (row(b, n), cq)),
            kspec(ck, -1), kspec(ck, 0), kspec(ck, 1),
            kspec(cv, -1), kspec(cv, 0), kspec(cv, 1),
            pl.BlockSpec((n_ctx, ATT_KV_W), lambda b, n: (b, xk)),
            pl.BlockSpec((n_ctx, ATT_KV_W), lambda b, n: (b, xv)),
            pl.BlockSpec((seq_len, HEAD_DIM), lambda b, n: (0, 0)),
            pl.BlockSpec((seq_len, HEAD_DIM), lambda b, n: (0, 0)),
        ],
        out_specs=pl.BlockSpec((blk, ATT_Q_W), lambda b, n: (row(b, n), 0)),
        compiler_params=_params("parallel", "arbitrary"),
        name="latent_attention",
    )(sink, p_lat, p_lat, p_lat, p_lat, p_lat, p_lat, p_lat, p_ctx, p_ctx, cos_t, sin_t)


def context_attention(p_ctx, sink, batch, n_ctx):
    cq = COL_QA // ATT_Q_W
    ck, cv = COL_KA // ATT_KV_W, COL_VA // ATT_KV_W
    return pl.pallas_call(
        functools.partial(_attn_kernel, banded=False, n_ctx=n_ctx, seq_len=n_ctx),
        out_shape=jax.ShapeDtypeStruct((batch * n_ctx, ATT_Q_W), BF16),
        grid=(batch, 1),
        in_specs=[
            pl.BlockSpec(memory_space=pltpu.SMEM),
            pl.BlockSpec((n_ctx, ATT_Q_W), lambda b, n: (b, cq)),
            pl.BlockSpec((n_ctx, ATT_KV_W), lambda b, n: (b, ck)),
            pl.BlockSpec((n_ctx, ATT_KV_W), lambda b, n: (b, cv)),
        ],
        out_specs=pl.BlockSpec((n_ctx, ATT_Q_W), lambda b, n: (b, 0)),
        compiler_params=_params("parallel", "arbitrary"),
        name="context_attention",
    )(sink, p_ctx, p_ctx, p_ctx)


def rope_tables(seq_len):
    rows = seq_len // GRID_W
    row = jnp.repeat(jnp.arange(rows, dtype=F32), GRID_W)
    col = jnp.tile(jnp.arange(GRID_W, dtype=F32), rows)
    half = HEAD_DIM // 2
    inv_freq = ROPE_BASE ** (-jnp.arange(0, half, 2, dtype=F32) / half)
    ang_r, ang_c = row[:, None] * inv_freq, col[:, None] * inv_freq
    cos_t = jnp.concatenate([jnp.cos(ang_r)] * 2 + [jnp.cos(ang_c)] * 2, axis=-1)
    sin_t = jnp.concatenate([-jnp.sin(ang_r), jnp.sin(ang_r), -jnp.sin(ang_c), jnp.sin(ang_c)], axis=-1)
    return cos_t, sin_t


def _fourier_kernel(u_ref, cc_ref, sc_ref, ct_ref, st_ref, o_ref, p_ref, q_ref, *, scale):
    @pl.when(pl.program_id(2) == 0)
    def _():
        u = u_ref[...].astype(BF16)
        p_ref[...] = jnp.dot(u, cc_ref[...], preferred_element_type=F32).astype(BF16)
        q_ref[...] = jnp.dot(u, sc_ref[...], preferred_element_type=F32).astype(BF16)

    re = (jnp.dot(ct_ref[...], p_ref[...], preferred_element_type=F32)
          - jnp.dot(st_ref[...], q_ref[...], preferred_element_type=F32))
    o_ref[...] = (re * scale).astype(o_ref.dtype)


def dft_tables(n):
    k = jnp.arange(n, dtype=jnp.int32)
    ang = ((k[:, None] * k[None, :]) % n).astype(F32) * (2.0 * math.pi / n)
    return jnp.cos(ang).astype(BF16), jnp.sin(ang).astype(BF16)


def fourier_mix(p2d, batch, seq_len, tr=512):
    tr = min(tr, seq_len)
    cc, sc = dft_tables(FT_GROUP_DIM)
    ct, st = dft_tables(seq_len)
    c0 = COL_FT // FT_GROUP_DIM
    nr = seq_len // tr
    return pl.pallas_call(
        functools.partial(_fourier_kernel, scale=1.0 / math.sqrt(seq_len * FT_GROUP_DIM)),
        out_shape=jax.ShapeDtypeStruct((batch * seq_len, BRANCH_WIDTH), BF16),
        grid=(batch, FT_GROUPS, nr),
        in_specs=[
            pl.BlockSpec((seq_len, FT_GROUP_DIM), lambda b, g, r: (b, c0 + g)),
            pl.BlockSpec((FT_GROUP_DIM, FT_GROUP_DIM), lambda b, g, r: (0, 0)),
            pl.BlockSpec((FT_GROUP_DIM, FT_GROUP_DIM), lambda b, g, r: (0, 0)),
            pl.BlockSpec((tr, seq_len), lambda b, g, r: (r, 0)),
            pl.BlockSpec((tr, seq_len), lambda b, g, r: (r, 0)),
        ],
        out_specs=pl.BlockSpec((tr, FT_GROUP_DIM), lambda b, g, r: (b * nr + r, g)),
        scratch_shapes=[pltpu.VMEM((seq_len, FT_GROUP_DIM), BF16)] * 2,
        compiler_params=_params("parallel", "parallel", "arbitrary"),
        name="fourier_mix",
    )(p2d, cc, sc, ct, st)


POOL_TILE = 256
POOL_HALO = 128


def _pool_kernel(u_ref, band_ref, w_ref, s_ref, o_ref, pad_ref, *, seq_len):
    g = pl.program_id(1)
    zeros = jnp.zeros((POOL_HALO, POOL_GROUP_DIM), BF16)
    pad_ref[0:POOL_HALO, :] = zeros
    pad_ref[POOL_HALO + seq_len:POOL_HALO + seq_len + POOL_HALO, :] = zeros
    pad_ref[POOL_HALO:POOL_HALO + seq_len, :] = u_ref[...].astype(BF16)
    half = jnp.left_shift(1, g)
    for t in range(seq_len // POOL_TILE):
        r0 = t * POOL_TILE
        win = pad_ref[r0:r0 + POOL_TILE + 2 * POOL_HALO, :]
        sums = jnp.dot(band_ref[...], win, preferred_element_type=F32)
        pos = r0 + lax.broadcasted_iota(jnp.int32, (POOL_TILE, 1), 0)
        cnt = jnp.minimum(pos + half, seq_len) - jnp.maximum(pos - half, 0)
        m = sums / cnt.astype(F32) - u_ref[r0:r0 + POOL_TILE, :]
        y = jnp.dot(m.astype(BF16), w_ref[...], preferred_element_type=F32) * s_ref[...]
        o_ref[r0:r0 + POOL_TILE, :] = y.astype(o_ref.dtype)


def pool_bands():
    r = np.arange(POOL_TILE)[:, None]
    c = np.arange(POOL_TILE + 2 * POOL_HALO)[None, :]
    rel = c - POOL_HALO - r
    bands = [((rel >= -(w // 2)) & (rel <= w // 2 - 1)) for w in POOL_WINDOWS]
    return jnp.asarray(np.stack(bands).astype(np.float32), dtype=BF16)


def pool_mix(p2d, pool_w, pool_scale, batch, seq_len):
    c0 = COL_POOL // POOL_GROUP_DIM
    ng = len(POOL_WINDOWS)
    return pl.pallas_call(
        functools.partial(_pool_kernel, seq_len=seq_len),
        out_shape=jax.ShapeDtypeStruct((batch * seq_len, BRANCH_WIDTH), BF16),
        grid=(batch, ng),
        in_specs=[
            pl.BlockSpec((seq_len, POOL_GROUP_DIM), lambda b, g: (b, c0 + g)),
            pl.BlockSpec((None, POOL_TILE, POOL_TILE + 2 * POOL_HALO), lambda b, g: (g, 0, 0)),
            pl.BlockSpec((None, POOL_GROUP_DIM, POOL_GROUP_DIM), lambda b, g: (g, 0, 0)),
            pl.BlockSpec((1, POOL_GROUP_DIM), lambda b, g: (0, g)),
        ],
        out_specs=pl.BlockSpec((seq_len, POOL_GROUP_DIM), lambda b, g: (b, g)),
        scratch_shapes=[pltpu.VMEM((seq_len + 2 * POOL_HALO, POOL_GROUP_DIM), BF16)],
        compiler_params=_params("parallel", "parallel"),
        name="pool_mix",
    )(p2d, pool_bands(), pool_w.astype(BF16), pool_scale.reshape(1, BRANCH_WIDTH))


def permute_projection_columns(w):
    o_ka, o_va, o_kd, o_vd, o_a, o_b = 0, 256, 512, 1536, 2560, 2576
    o_qa, o_qd, o_zd, o_ft, o_pool, o_gate = 2592, 3616, 4640, 5664, 6688, 7712
    seg = lambda o, n: w[..., o:o + n]
    parts = [seg(o_gate, GATE_W), seg(o_qa, ATT_Q_W), seg(o_qd, DN_WIDTH), seg(o_kd, DN_WIDTH),
             seg(o_vd, DN_WIDTH), seg(o_zd, DN_WIDTH), seg(o_ft, BRANCH_WIDTH), seg(o_pool, BRANCH_WIDTH),
             seg(o_ka, ATT_KV_W), seg(o_va, ATT_KV_W), seg(o_a, 2 * DN_HEADS), seg(o_b, 2 * DN_HEADS)]
    used = sum(p.shape[-1] for p in parts)
    parts.append(jnp.zeros(w.shape[:-1] + (COLS_PAD - used,), w.dtype))
    return jnp.concatenate(parts, axis=-1)


CONV_PAD = 8
N_LEVELS = 6


def dn_masks():
    i = np.arange(DN_SUPER)[:, None]
    j = np.arange(DN_SUPER)[None, :]
    incl, lev = [], []
    for later in (lambda a, b: a >= b, lambda a, b: a <= b):
        strict = later(i, j) & (i != j)
        incl.append((i // DN_CHUNK == j // DN_CHUNK) & later(i, j))
        lv = [(i // 2 == j // 2) & strict]
        sz = 2
        while sz < DN_CHUNK:
            lv.append((i // (2 * sz) == j // (2 * sz)) & (i // sz != j // sz) & strict)
            sz *= 2
        lev.append(np.stack(lv))
    return (jnp.asarray(np.stack(incl).astype(np.float32)), jnp.asarray(np.stack(lev).astype(np.float32)))


def _split3(x):
    x1 = x.astype(BF16)
    r1 = x - x1.astype(F32)
    x2 = r1.astype(BF16)
    x3 = (r1 - x2.astype(F32)).astype(BF16)
    return x1, x2, x3


def _softplus(x):
    return jnp.maximum(x, 0.0) + jnp.log1p(jnp.exp(-jnp.abs(x)))


def _dn_kernel(alog_ref, dtb_ref, q_ref, k_ref, v_ref, z_ref, ab_ref, cq_ref, ck_ref, cv_ref, nw_ref,
               s0_ref, incl_ref, lev_ref, o_ref, sfin_ref,
               pq_ref, pk_ref, pv_ref, qe_ref, g_ref, b_ref, eg_ref, oacc_ref, *, seq_len, with_q):
    h = pl.program_id(1)
    n_super = seq_len // DN_SUPER
    n_chunks = seq_len // DN_CHUNK
    per_super = DN_SUPER // DN_CHUNK
    zpad = jnp.zeros((CONV_PAD, HEAD_DIM), F32)
    streams = [(pk_ref, k_ref), (pv_ref, v_ref)] + ([(pq_ref, q_ref)] if with_q else [])
    for pad_ref, src_ref in streams:
        pad_ref[0:CONV_PAD, :] = zpad
        pad_ref[CONV_PAD + seq_len:2 * CONV_PAD + seq_len, :] = zpad
        pad_ref[CONV_PAD:CONV_PAD + seq_len, :] = src_ref[...]

    lane = lax.broadcasted_iota(jnp.int32, (1, HEAD_DIM), 1)
    pick = lambda ref: jnp.where(lane == 0, ref[0, h], jnp.where(lane == 1, ref[1, h], 0.0))
    neg_a = -jnp.exp(pick(alog_ref))
    dtb = pick(dtb_ref)
    eye = (lax.broadcasted_iota(jnp.int32, (DN_SUPER, DN_SUPER), 0)
           == lax.broadcasted_iota(jnp.int32, (DN_SUPER, DN_SUPER), 1)).astype(F32)
    half = DN_CONV // 2

    def conv_silu(pad_ref, cw_ref, r0):
        win = pad_ref[pl.ds(r0, DN_SUPER + 2 * CONV_PAD), :]
        acc = None
        for j in range(DN_CONV):
            off = CONV_PAD - half + j
            term = cw_ref[j:j + 1, :] * win[off:off + DN_SUPER, :]
            acc = term if acc is None else acc + term
        return _silu(acc)

    def l2n(x):
        return x * lax.rsqrt(jnp.sum(x * x, axis=-1, keepdims=True) + EPS)

    def chain_setup(sidx):
        r0 = pl.multiple_of(sidx * DN_SUPER, DN_SUPER)
        k = l2n(conv_silu(pk_ref, ck_ref, r0))
        v = conv_silu(pv_ref, cv_ref, r0)
        kb = k.astype(BF16)
        kk = lax.dot_general(kb, kb, (((1,), (1,)), ((), ())), preferred_element_type=F32)
        q = qk = None
        if with_q:
            q = l2n(conv_silu(pq_ref, cq_ref, r0)) * (HEAD_DIM ** -0.5)
            qk = lax.dot_general(q.astype(BF16), kb, (((1,), (1,)), ((), ())), preferred_element_type=F32)
        ab = ab_ref[pl.ds(r0, DN_SUPER), :]
        gt = neg_a * _softplus(ab + dtb)
        bt = jax.nn.sigmoid(ab)
        gparts = _split3(gt)
        gcs = []
        for d in range(2):
            seg = incl_ref[d].astype(BF16)
            gcs.append(sum(jnp.dot(seg, gp, preferred_element_type=F32) for gp in gparts))
        gtot_t = gcs[0] + gcs[1] - gt
        chains = []
        for d in range(2):
            gcol = gcs[d][:, d:d + 1]
            grow = jnp.transpose(gcs[d])[d:d + 1, :]
            beta = bt[:, 2 + d:3 + d]
            e = jnp.exp((gcol - grow) * incl_ref[d])
            nf = (beta * kk) * e
            chains.append(dict(d=d, sidx=sidx, r0=r0, k=k, v=v, q=q, qk=qk, gcol=gcol, gtot=gtot_t[:, d:d + 1],
                               beta=beta, e=e, nf=nf, t=eye - nf * lev_ref[d, 0]))
        return chains

    def chain_finish(ch):
        d, k, gcol, gtot, beta = ch["d"], ch["k"], ch["gcol"], ch["gtot"], ch["beta"]
        egc = jnp.exp(gcol)
        rhs = jnp.concatenate([beta * ch["v"], (beta * egc) * k], axis=1).astype(BF16)
        sol = jnp.dot(ch["t"].astype(BF16), rhs, preferred_element_type=F32)
        u, w = sol[:, :HEAD_DIM], sol[:, HEAD_DIM:]
        wu = jnp.concatenate([w, u], axis=1).astype(BF16)
        kd = (k * jnp.exp(gtot - gcol)).astype(BF16)
        if with_q:
            qkm = (ch["qk"] * ch["e"] * incl_ref[d]).astype(BF16)
            r2 = jnp.dot(qkm, wu, preferred_element_type=F32)
            rows = pl.ds(ch["r0"], DN_SUPER)
            qe_ref[d, rows, :] = (ch["q"] * egc - r2[:, :HEAD_DIM]).astype(BF16)
            oacc_ref[d, rows, :] = r2[:, HEAD_DIM:]
        eg = jnp.exp(gtot)
        for c in range(per_super):
            cs = slice(c * DN_CHUNK, (c + 1) * DN_CHUNK)
            gb = lax.dot_general(kd[cs], wu[cs], (((0,), (0,)), ((), ())), preferred_element_type=F32)
            chunk = ch["sidx"] * per_super + c
            srows = pl.ds(pl.multiple_of(chunk * HEAD_DIM, HEAD_DIM), HEAD_DIM)
            g_ref[d, srows, :] = gb[:, :HEAD_DIM].astype(BF16)
            b_ref[d, srows, :] = gb[:, HEAD_DIM:]
            blk = jnp.broadcast_to(eg[c * DN_CHUNK:c * DN_CHUNK + 8, :], (8, HEAD_DIM))
            eg_ref[d, pl.ds(pl.multiple_of(chunk * 8, 8), 8), :] = blk

    group = 2 if n_super % 2 == 0 else 1

    def super_group(gi, carry):
        chains = [ch for s in range(group) for ch in chain_setup(gi * group + s)]
        for m in range(1, N_LEVELS):
            for ch in chains:
                ch["tb"] = ch["t"].astype(BF16)
                ch["x"] = jnp.dot((ch["nf"] * lev_ref[ch["d"], m]).astype(BF16), ch["tb"],
                                  preferred_element_type=F32)
            for ch in chains:
                ch["t"] = ch["t"] - jnp.dot(ch["tb"], ch["x"].astype(BF16), preferred_element_type=F32)
        for ch in chains:
            chain_finish(ch)
        return carry

    lax.fori_loop(0, n_super // group, super_group, 0)

    def chunk_step(c, states):
        ccs = (c, n_chunks - 1 - c)
        prods = []
        for d in range(2):
            srows = pl.ds(pl.multiple_of(ccs[d] * HEAD_DIM, HEAD_DIM), HEAD_DIM)
            lhs = g_ref[d, srows, :]
            if with_q:
                rows = pl.ds(pl.multiple_of(ccs[d] * DN_CHUNK, DN_CHUNK), DN_CHUNK)
                lhs = jnp.concatenate([lhs, qe_ref[d, rows, :]], axis=0)
            prods.append(jnp.dot(lhs, states[d].astype(BF16), preferred_element_type=F32))
        new_states = []
        for d in range(2):
            srows = pl.ds(pl.multiple_of(ccs[d] * HEAD_DIM, HEAD_DIM), HEAD_DIM)
            decay = eg_ref[d, pl.ds(pl.multiple_of(ccs[d] * 8, 8), 8), :][0:1, :]
            new_states.append(decay * states[d] - prods[d][:HEAD_DIM] + b_ref[d, srows, :])
            if with_q:
                rows = pl.ds(pl.multiple_of(ccs[d] * DN_CHUNK, DN_CHUNK), DN_CHUNK)
                oacc_ref[d, rows, :] += prods[d][HEAD_DIM:]
        return tuple(new_states)

    s_f, s_b = lax.fori_loop(0, n_chunks, chunk_step, (s0_ref[0], s0_ref[1]))
    sfin_ref[0] = s_f
    sfin_ref[1] = s_b
    if with_q:
        o = oacc_ref[0] + oacc_ref[1]
        on = o * lax.rsqrt(jnp.mean(o * o, axis=-1, keepdims=True) + EPS) * nw_ref[...]
        o_ref[...] = (on * _silu(z_ref[...])).astype(o_ref.dtype)
    else:
        o_ref[...] = jnp.zeros(o_ref.shape, o_ref.dtype)


def head_gate_layout(p2d, col0=0):
    m = p2d.shape[0]
    c = COL_AB - col0
    ab = p2d[:, c:c + 4 * DN_HEADS].astype(F32).reshape(m, 4, DN_HEADS)
    ab = jnp.transpose(ab, (0, 2, 1))
    ab = jnp.pad(ab, ((0, 0), (0, 0), (0, LANES - 4)))
    return ab.reshape(m, DN_HEADS * LANES)


def deltanet_mix(p2d, a_log, dt_bias, conv_w, norm_w, s0, batch, seq_len, with_q=True, col0=0):
    incl, lev = dn_masks()
    ab = head_gate_layout(p2d, col0)
    cols = (COL_QD, COL_KD, COL_VD, COL_ZD) if with_q else (COL_KD, COL_KD, COL_VD, COL_KD)
    cqb, ckb, cvb, czb = ((c - col0) // HEAD_DIM for c in cols)
    seq_spec = lambda c0: pl.BlockSpec((seq_len, HEAD_DIM), lambda b, h: (b, c0 + h))
    conv_spec = lambda c0: pl.BlockSpec((DN_CONV, HEAD_DIM), lambda b, h: (0, c0 + h))
    state_spec = pl.BlockSpec((None, None, 2, HEAD_DIM, HEAD_DIM), lambda b, h: (b, h, 0, 0, 0))
    smem = pl.BlockSpec(memory_space=pltpu.SMEM)
    n_chunks = seq_len // DN_CHUNK
    per_dir = lambda dt: pltpu.VMEM((2, seq_len, HEAD_DIM), dt)
    pad = pltpu.VMEM((seq_len + 2 * CONV_PAD, HEAD_DIM), F32)
    return pl.pallas_call(
        functools.partial(_dn_kernel, seq_len=seq_len, with_q=with_q),
        out_shape=(jax.ShapeDtypeStruct((batch * seq_len, DN_WIDTH), BF16),
                   jax.ShapeDtypeStruct((batch, DN_HEADS, 2, HEAD_DIM, HEAD_DIM), F32)),
        grid=(batch, DN_HEADS),
        in_specs=[
            smem, smem,
            seq_spec(cqb), seq_spec(ckb), seq_spec(cvb), seq_spec(czb),
            pl.BlockSpec((seq_len, LANES), lambda b, h: (b, h)),
            conv_spec(0), conv_spec(DN_HEADS), conv_spec(2 * DN_HEADS),
            pl.BlockSpec((1, HEAD_DIM), lambda b, h: (0, 0)),
            state_spec,
            pl.BlockSpec((2, DN_SUPER, DN_SUPER), lambda b, h: (0, 0, 0)),
            pl.BlockSpec((2, N_LEVELS, DN_SUPER, DN_SUPER), lambda b, h: (0, 0, 0, 0)),
        ],
        out_specs=(pl.BlockSpec((seq_len, HEAD_DIM), lambda b, h: (b, h)), state_spec),
        scratch_shapes=[pad, pad, pad, per_dir(BF16),
                        pltpu.VMEM((2, n_chunks * HEAD_DIM, HEAD_DIM), BF16),
                        pltpu.VMEM((2, n_chunks * HEAD_DIM, HEAD_DIM), F32),
                        pltpu.VMEM((2, n_chunks * 8, HEAD_DIM), F32), per_dir(F32)],
        compiler_params=_params("parallel", "parallel"),
        name="deltanet_mix",
    )(a_log, dt_bias, p2d, p2d, p2d, p2d, ab, conv_w, conv_w, conv_w, norm_w.reshape(1, HEAD_DIM),
      s0, incl, lev)


def _merge_gate_kernel(y0, y1, y2, y3, g0, g1, g2, g3, wb_ref, o_ref):
    acc = None
    for n, (y_ref, g_ref) in enumerate(((y0, g0), (y1, g1), (y2, g2), (y3, g3))):
        t = jnp.dot(y_ref[...], wb_ref[n], preferred_element_type=F32) * jax.nn.sigmoid(g_ref[...].astype(F32))
        acc = t if acc is None else acc + t
    o_ref[...] = acc.astype(o_ref.dtype)


def merge_gate(branches, p2d, wb, tm=512, tn=1024):
    m = p2d.shape[0]
    d = wb.shape[-1]
    tm = min(tm, m)
    nj = d // tn
    y_spec = pl.BlockSpec((tm, BRANCH_WIDTH), lambda i, j: (i, 0))
    g_spec = lambda n: pl.BlockSpec((tm, tn), lambda i, j: (i, COL_GATE // tn + n * nj + j))
    return pl.pallas_call(
        _merge_gate_kernel,
        out_shape=jax.ShapeDtypeStruct((m, d), BF16),
        grid=(m // tm, nj),
        in_specs=[y_spec] * N_BRANCHES + [g_spec(n) for n in range(N_BRANCHES)]
        + [pl.BlockSpec((N_BRANCHES, BRANCH_WIDTH, tn), lambda i, j: (0, 0, j))],
        out_specs=pl.BlockSpec((tm, tn), lambda i, j: (i, j)),
        compiler_params=_params("parallel", "arbitrary"),
        name="merge_gate",
    )(*branches, p2d, p2d, p2d, p2d, wb)


def _out_proj_kernel(a_ref, w_ref, x_ref, g_ref, o_ref):
    o_ref[...] = x_ref[...] + g_ref[...] * jnp.dot(a_ref[...], w_ref[...], preferred_element_type=F32)


def out_project_residual(a, w, x2d, gate, rows_per_mod, tm=512, tn=1024):
    m, d = x2d.shape
    tm = min(tm, m)
    return pl.pallas_call(
        _out_proj_kernel,
        out_shape=jax.ShapeDtypeStruct((m, d), F32),
        grid=(m // tm, d // tn),
        in_specs=[
            pl.BlockSpec((tm, a.shape[1]), lambda i, j: (i, 0)),
            pl.BlockSpec((a.shape[1], tn), lambda i, j: (0, j)),
            pl.BlockSpec((tm, tn), lambda i, j: (i, j)),
            pl.BlockSpec((None, 1, tn), lambda i, j: ((i * tm) // rows_per_mod, 0, j)),
        ],
        out_specs=pl.BlockSpec((tm, tn), lambda i, j: (i, j)),
        compiler_params=_params("parallel", "arbitrary"),
        name="out_project_residual",
    )(a, w, x2d, gate)


def _first_argmax(v, lane):
    m = jnp.max(v, axis=-1, keepdims=True)
    idx = jnp.min(jnp.where(v == m, lane, LANES), axis=-1, keepdims=True)
    return m, idx


HI_MASK = 0xFFFF0000


def _pack_halves(v):
    half = v.shape[1] // 2
    lo = pltpu.bitcast(v[:, :half].astype(BF16).astype(F32), jnp.uint32)
    hi = pltpu.bitcast(v[:, half:].astype(BF16).astype(F32), jnp.uint32)
    return hi | (lo >> 16)


def _unpack_halves(p):
    return pltpu.bitcast(p << 16, F32), pltpu.bitcast(p & jnp.uint32(HI_MASK), F32)


def _route_kernel(x_ref, g_ref, sc_ref, sh_ref, rw_ref, rb_ref, h_ref, ids_ref, wts_ref):
    h = _modulated_norm(x_ref[...], g_ref[...], sc_ref[...], sh_ref[...])
    h_ref[...] = _pack_halves(h)
    logits = jnp.dot(h, rw_ref[...], preferred_element_type=F32, precision=lax.Precision.HIGHEST)
    scores = jax.nn.sigmoid(logits)
    lane = lax.broadcasted_iota(jnp.int32, scores.shape, 1)
    grp = lane // (N_EXPERTS // N_EXPERT_GROUPS)
    ninf = -jnp.inf
    vb = jnp.where(lane < N_EXPERTS, scores + rb_ref[...], ninf)
    gs = jnp.full(scores.shape, ninf, F32)
    for g in range(N_EXPERT_GROUPS):
        vg = jnp.where(grp == g, vb, ninf)
        m1, i1 = _first_argmax(vg, lane)
        m2 = jnp.max(jnp.where(lane == i1, ninf, vg), axis=-1, keepdims=True)
        gs = jnp.where(lane == g, m1 + m2, gs)
    allowed = jnp.zeros(scores.shape, F32)
    for _ in range(TOPK_GROUPS):
        _, gi = _first_argmax(gs, lane)
        gs = jnp.where(lane == gi, ninf, gs)
        allowed = jnp.where(grp == gi, 1.0, allowed)
    ve = jnp.where(allowed > 0, vb, ninf)
    ids = jnp.zeros(scores.shape, jnp.int32)
    wts = jnp.zeros(scores.shape, F32)
    for k in range(TOP_K):
        _, ei = _first_argmax(ve, lane)
        hit = lane == ei
        sk = jnp.sum(jnp.where(hit, scores, 0.0), axis=-1, keepdims=True)
        ids = jnp.where(lane == k, ei, ids)
        wts = jnp.where(lane == k, sk, wts)
        ve = jnp.where(hit, ninf, ve)
    ids_ref[...] = ids
    wts_ref[...] = wts / jnp.sum(wts, axis=-1, keepdims=True) * ROUTE_SCALE


def route(x2d, norm_w, sc1p, sh, router_w, router_bias, rows_per_mod, tm=256):
    m, d = x2d.shape
    tm = min(tm, m)
    rw = jnp.pad(router_w, ((0, 0), (0, LANES - N_EXPERTS)))
    rb = jnp.pad(router_bias, (0, LANES - N_EXPERTS)).reshape(1, LANES)
    mod_idx = lambda i: ((i * tm) // rows_per_mod, 0, 0)
    return pl.pallas_call(
        _route_kernel,
        out_shape=(jax.ShapeDtypeStruct((m, d // 2), jnp.uint32), jax.ShapeDtypeStruct((m, LANES), jnp.int32),
                   jax.ShapeDtypeStruct((m, LANES), F32)),
        grid=(m // tm,),
        in_specs=[
            pl.BlockSpec((tm, d), lambda i: (i, 0)),
            pl.BlockSpec((1, d), lambda i: (0, 0)),
            pl.BlockSpec((None, 1, d), mod_idx),
            pl.BlockSpec((None, 1, d), mod_idx),
            pl.BlockSpec((d, LANES), lambda i: (0, 0)),
            pl.BlockSpec((1, LANES), lambda i: (0, 0)),
        ],
        out_specs=(pl.BlockSpec((tm, d // 2), lambda i: (i, 0)), pl.BlockSpec((tm, LANES), lambda i: (i, 0)),
                   pl.BlockSpec((tm, LANES), lambda i: (i, 0))),
        compiler_params=_params("parallel"),
        name="route",
    )(x2d, norm_w.reshape(1, d), sc1p, sh, rw, rb)


def _half_k_dot(lo, hi, w_ref):
    half = lo.shape[1]
    return (jnp.dot(lo, w_ref[0:half, :], preferred_element_type=F32)
            + jnp.dot(hi, w_ref[half:, :], preferred_element_type=F32))


def _swiglu_packed(p, wg_ref, wu_ref, wd_ref):
    lo, hi = (v.astype(BF16) for v in _unpack_halves(p))
    hm = (_silu(_half_k_dot(lo, hi, wg_ref)) * _half_k_dot(lo, hi, wu_ref)).astype(BF16)
    return jnp.dot(hm, wd_ref[...], preferred_element_type=F32)


def _shared_ffn_kernel(h_ref, wg_ref, wu_ref, wd_ref, o_ref, acc_ref, wgb, wub, wdb):
    e = pl.program_id(1)
    wgb[...] = wg_ref[...].astype(BF16)
    wub[...] = wu_ref[...].astype(BF16)
    wdb[...] = wd_ref[...].astype(BF16)
    part = _swiglu_packed(h_ref[...], wgb, wub, wdb)

    @pl.when(e == 0)
    def _():
        acc_ref[...] = part

    @pl.when(e > 0)
    def _():
        acc_ref[...] += part

    @pl.when(e == pl.num_programs(1) - 1)
    def _():
        o_ref[...] = acc_ref[...].astype(o_ref.dtype)


def shared_expert(h2p, wg, wu, wd, tm=512):
    m = h2p.shape[0]
    d, f = wg.shape
    tm = min(tm, m)
    fe = EXPERT_DIM
    return pl.pallas_call(
        _shared_ffn_kernel,
        out_shape=jax.ShapeDtypeStruct((m, d), BF16),
        grid=(m // tm, f // fe),
        in_specs=[
            pl.BlockSpec((tm, d // 2), lambda i, e: (i, 0)),
            pl.BlockSpec((d, fe), lambda i, e: (0, e)),
            pl.BlockSpec((d, fe), lambda i, e: (0, e)),
            pl.BlockSpec((fe, d), lambda i, e: (e, 0)),
        ],
        out_specs=pl.BlockSpec((tm, d), lambda i, e: (i, 0)),
        scratch_shapes=[pltpu.VMEM((tm, d), F32), pltpu.VMEM((d, fe), BF16), pltpu.VMEM((d, fe), BF16),
                        pltpu.VMEM((fe, d), BF16)],
        compiler_params=_params("parallel", "arbitrary"),
        name="shared_expert",
    )(h2p, wg, wu, wd)


EXPERT_TILE = 256
COMBINE_TILE = 64


def dispatch_plan(ids):
    n = ids.shape[0]
    a = n * TOP_K
    tm = EXPERT_TILE
    n_tiles = a // tm + N_EXPERTS
    flat = ids.reshape(a)
    order = jnp.argsort(flat, stable=True).astype(jnp.int32)
    sorted_e = flat[order]
    counts = jnp.zeros((N_EXPERTS,), jnp.int32).at[flat].add(1)
    tiles_per = (counts + tm - 1) // tm
    tile_end = jnp.cumsum(tiles_per)
    row_start = (tile_end - tiles_per) * tm
    first = jnp.cumsum(counts) - counts
    rows = row_start[sorted_e] + jnp.arange(a, dtype=jnp.int32) - first[sorted_e]
    row_tok = jnp.zeros((n_tiles * tm,), jnp.int32).at[rows].set(order // TOP_K)
    row_of = jnp.zeros((a,), jnp.int32).at[order].set(rows)
    n_used = tile_end[-1:]
    tile_e = jnp.minimum(jnp.searchsorted(tile_end, jnp.arange(n_tiles, dtype=jnp.int32), side="right"),
                         N_EXPERTS - 1).astype(jnp.int32)
    last_e = tile_e[jnp.maximum(n_used[0] - 1, 0)]
    tile_e = jnp.where(jnp.arange(n_tiles) < n_used[0], tile_e, last_e)
    return row_tok.reshape(n_tiles, 1, tm), tile_e, n_used.astype(jnp.int32), row_of.reshape(n, TOP_K)


def _row_gather(idx_ref, n_rows, src_ref, dst_at, sem):
    def body(r, carry):
        t = idx_ref[0, r]
        pltpu.make_async_copy(src_ref.at[pl.ds(t, 1), :], dst_at(r), sem).start()
        return carry
    lax.fori_loop(0, n_rows, body, 0, unroll=8)


def _expert_kernel(te_ref, nu_ref, cur_ref, nxt_ref, h_ref, wg_ref, wu_ref, wd_ref, o_ref,
                   xbuf, wgb, wub, wdb, sem):
    j = pl.program_id(0)
    n_used = nu_ref[0]
    slot = j % 2
    tm = xbuf.shape[1]

    @pl.when(j == 0)
    def _():
        _row_gather(cur_ref, tm, h_ref, lambda r: xbuf.at[0, pl.ds(r, 1), :], sem.at[0])

    @pl.when(j + 1 < n_used)
    def _():
        _row_gather(nxt_ref, tm, h_ref, lambda r: xbuf.at[1 - slot, pl.ds(r, 1), :], sem.at[1 - slot])

    @pl.when(j < n_used)
    def _():
        @pl.when((j == 0) | (te_ref[j] != te_ref[jnp.maximum(j - 1, 0)]))
        def _():
            wgb[...] = wg_ref[...].astype(BF16)
            wub[...] = wu_ref[...].astype(BF16)
            wdb[...] = wd_ref[...].astype(BF16)

        pltpu.make_async_copy(xbuf.at[slot], xbuf.at[slot], sem.at[slot]).wait()
        o_ref[...] = _pack_halves(_swiglu_packed(xbuf[slot], wgb, wub, wdb))

    @pl.when(j >= n_used)
    def _():
        o_ref[...] = jnp.zeros(o_ref.shape, o_ref.dtype)


def routed_experts(h2p, row_tok, tile_e, n_used, wg, wu, wd):
    n_tiles, _, tm = row_tok.shape
    ne, d, f = wg.shape
    last = n_tiles - 1
    idx_spec = lambda shift: pl.BlockSpec((None, 1, tm), lambda j, te, nu: (jnp.minimum(j + shift, last), 0, 0),
                                          memory_space=pltpu.SMEM)
    return pl.pallas_call(
        _expert_kernel,
        out_shape=jax.ShapeDtypeStruct((n_tiles * tm, d // 2), jnp.uint32),
        grid_spec=pltpu.PrefetchScalarGridSpec(
            num_scalar_prefetch=2,
            grid=(n_tiles,),
            in_specs=[
                idx_spec(0), idx_spec(1),
                pl.BlockSpec(memory_space=pl.ANY),
                pl.BlockSpec((None, d, f), lambda j, te, nu: (te[j], 0, 0)),
                pl.BlockSpec((None, d, f), lambda j, te, nu: (te[j], 0, 0)),
                pl.BlockSpec((None, f, d), lambda j, te, nu: (te[j], 0, 0)),
            ],
            out_specs=pl.BlockSpec((tm, d // 2), lambda j, te, nu: (j, 0)),
            scratch_shapes=[pltpu.VMEM((2, tm, d // 2), jnp.uint32), pltpu.VMEM((d, f), BF16),
                            pltpu.VMEM((d, f), BF16), pltpu.VMEM((f, d), BF16), pltpu.SemaphoreType.DMA((2,))],
        ),
        compiler_params=_params("arbitrary"),
        name="routed_experts",
    )(tile_e, n_used, row_tok, row_tok, h2p, wg, wu, wd)


def _combine_kernel(cur_ref, nxt_ref, ys_ref, w_ref, ysh_ref, x_ref, g_ref, nf_ref, o_ref, gbuf, sem, *,
                    final_norm):
    i = pl.program_id(0)
    slot = i % 2
    tmc = x_ref.shape[0]
    dst = lambda s: (lambda a: gbuf.at[s, a % TOP_K, pl.ds(a // TOP_K, 1), :])

    @pl.when(i == 0)
    def _():
        _row_gather(cur_ref, tmc * TOP_K, ys_ref, dst(0), sem.at[0])

    @pl.when(i + 1 < pl.num_programs(0))
    def _():
        _row_gather(nxt_ref, tmc * TOP_K, ys_ref, dst(1 - slot), sem.at[1 - slot])

    pltpu.make_async_copy(gbuf.at[slot], gbuf.at[slot], sem.at[slot]).wait()
    w = w_ref[...]
    acc_lo = acc_hi = None
    for k in range(TOP_K):
        lo, hi = _unpack_halves(gbuf[slot, k])
        wk = w[:, k:k + 1]
        acc_lo = wk * lo if acc_lo is None else acc_lo + wk * lo
        acc_hi = wk * hi if acc_hi is None else acc_hi + wk * hi
    y = jnp.concatenate([acc_lo, acc_hi], axis=1) + ysh_ref[...].astype(F32)
    x = x_ref[...] + g_ref[...] * y
    if final_norm:
        x = x * lax.rsqrt(jnp.mean(x * x, axis=-1, keepdims=True) + EPS) * nf_ref[...]
    o_ref[...] = x


def combine_residual(x2d, ysh, ys, row_of, wts, gate, norm_final, rows_per_mod, row0, final_norm):
    m, d = x2d.shape
    tmc = COMBINE_TILE
    n_steps = m // tmc
    b0 = row0 // tmc
    idx = row_of.reshape(-1, 1, tmc * TOP_K)
    idx_spec = lambda shift: pl.BlockSpec(
        (None, 1, tmc * TOP_K), lambda i: (b0 + jnp.minimum(i + shift, n_steps - 1), 0, 0), memory_space=pltpu.SMEM)
    return pl.pallas_call(
        functools.partial(_combine_kernel, final_norm=final_norm),
        out_shape=jax.ShapeDtypeStruct((m, d), F32),
        grid=(n_steps,),
        in_specs=[
            idx_spec(0), idx_spec(1),
            pl.BlockSpec(memory_space=pl.ANY),
            pl.BlockSpec((tmc, LANES), lambda i: (b0 + i, 0)),
            pl.BlockSpec((tmc, d), lambda i: (b0 + i, 0)),
            pl.BlockSpec((tmc, d), lambda i: (i, 0)),
            pl.BlockSpec((None, 1, d), lambda i: ((i * tmc) // rows_per_mod, 0, 0)),
            pl.BlockSpec((1, d), lambda i: (0, 0)),
        ],
        out_specs=pl.BlockSpec((tmc, d), lambda i: (i, 0)),
        scratch_shapes=[pltpu.VMEM((2, TOP_K, tmc, d // 2), jnp.uint32), pltpu.SemaphoreType.DMA((2,))],
        compiler_params=_params("arbitrary"),
        name="combine_residual",
    )(idx, idx, ys, wts, ysh, x2d, gate, norm_final.reshape(1, d))


def kernel(x, c, ctx, c_ctx, ada_w, ada_b, norm_mix, norm_ffn, w_in, dn_conv, dn_a_log, dn_dt_bias, dn_norm, attn_sink, pool_w, pool_scale, w_branch, w_out, router_w, router_bias, exp_gate, exp_up, exp_down, shared_gate, shared_up, shared_down, norm_final):
    batch, seq_len, d = x.shape
    n_ctx = ctx.shape[1]
    depth = ada_w.shape[0]
    cond_rows = 16
    cond = jnp.zeros((cond_rows, d), F32).at[:batch].set(c).at[batch].set(c_ctx)
    mod_all = ada_modulation(cond, ada_w, ada_b).reshape(depth, cond_rows, 6, 1, d)
    cos_t, sin_t = rope_tables(seq_len)
    xl = x.reshape(batch * seq_len, d)
    xc = ctx.reshape(batch * n_ctx, d)
    zero_state = jnp.zeros((batch, DN_HEADS, 2, HEAD_DIM, HEAD_DIM), F32)
    for i in range(depth):
        last = i == depth - 1
        sh1, sc1, g1, sh2, sc2, g2 = (mod_all[i, :, s] for s in range(6))
        lat = lambda m: m[:batch]
        cx = lambda m: m[batch:batch + 1]
        w_perm = permute_projection_columns(w_in[i]).astype(BF16)
        wb = w_branch[i].astype(BF16)
        wo = w_out[i].astype(BF16)
        ctx_col0 = COL_KD if last else 0
        p_ctx = norm_project(xc, norm_mix[i], 1.0 + cx(sc1), cx(sh1), w_perm[:, ctx_col0:], batch * n_ctx)
        p_lat = norm_project(xl, norm_mix[i], 1.0 + lat(sc1), lat(sh1), w_perm, seq_len)

        y_dn_c, s_ctx = deltanet_mix(p_ctx, dn_a_log[i], dn_dt_bias[i], dn_conv[i], dn_norm[i], zero_state,
                                     batch, n_ctx, with_q=not last, col0=ctx_col0)
        if not last:
            branches_c = (fourier_mix(p_ctx, batch, n_ctx), y_dn_c,
                          pool_mix(p_ctx, pool_w[i], pool_scale[i], batch, n_ctx),
                          context_attention(p_ctx, attn_sink[i], batch, n_ctx))
            xc = out_project_residual(merge_gate(branches_c, p_ctx, wb), wo, xc, cx(g1), batch * n_ctx)

        y_dn, _ = deltanet_mix(p_lat, dn_a_log[i], dn_dt_bias[i], dn_conv[i], dn_norm[i], s_ctx, batch, seq_len)
        branches = (fourier_mix(p_lat, batch, seq_len), y_dn,
                    pool_mix(p_lat, pool_w[i], pool_scale[i], batch, seq_len),
                    latent_attention(p_lat, p_ctx, attn_sink[i], cos_t, sin_t, batch, seq_len, n_ctx, ctx_col0))
        xl = out_project_residual(merge_gate(branches, p_lat, wb), wo, xl, lat(g1), seq_len)

        h2p, ids, wts = route(xl, norm_ffn[i], 1.0 + lat(sc2), lat(sh2), router_w[i], router_bias[i], seq_len)
        if not last:
            routed_c = route(xc, norm_ffn[i], 1.0 + cx(sc2), cx(sh2), router_w[i], router_bias[i], batch * n_ctx)
            h2p, ids, wts = (jnp.concatenate(pair, axis=0) for pair in zip((h2p, ids, wts), routed_c))
        row_tok, tile_e, n_used, row_of = dispatch_plan(ids[:, :TOP_K])
        ys = routed_experts(h2p, row_tok, tile_e, n_used, exp_gate[i], exp_up[i], exp_down[i])
        ysh = shared_expert(h2p, shared_gate[i], shared_up[i], shared_down[i])
        n_lat = batch * seq_len
        if not last:
            xc = combine_residual(xc, ysh, ys, row_of, wts, cx(g2), norm_final, batch * n_ctx, n_lat, False)
        xl = combine_residual(xl, ysh, ys, row_of, wts, lat(g2), norm_final, seq_len, 0, last)
    return xl.reshape(batch, seq_len, d)
```

```python
import functools
import math

import jax
import jax.numpy as jnp
import numpy as np
from jax import lax
from jax.experimental import pallas as pl
from jax.experimental.pallas import tpu as pltpu

F32 = jnp.float32
BF16 = jnp.bfloat16

HEAD_DIM = 128
GRID_W = 64
ROPE_BASE = 10000.0
EPS = 1e-6
NEG_INF = -1e30
N_BRANCHES = 4
BRANCH_WIDTH = 1024
ATT_Q_HEADS = 8
ATT_KV_HEADS = 2
ATT_GROUP = ATT_Q_HEADS // ATT_KV_HEADS
WINDOW = 128
ATT_BLOCK = 128
DN_HEADS = 8
DN_WIDTH = DN_HEADS * HEAD_DIM
DN_CONV = 5
DN_CHUNK = 64
DN_SUPER = 256
FT_GROUPS = 4
FT_GROUP_DIM = BRANCH_WIDTH // FT_GROUPS
POOL_WINDOWS = (2, 4, 8, 16)
POOL_GROUP_DIM = BRANCH_WIDTH // len(POOL_WINDOWS)
N_EXPERTS = 64
N_EXPERT_GROUPS = 8
TOPK_GROUPS = 4
TOP_K = 8
EXPERT_DIM = 256
ROUTE_SCALE = 2.5

VMEM_LIMIT_BYTES = 56 * 1024 * 1024
LANES = 128

ATT_KV_W = ATT_KV_HEADS * HEAD_DIM
ATT_Q_W = ATT_Q_HEADS * HEAD_DIM
GATE_W = 16384
COL_GATE = 0
COL_QA = COL_GATE + GATE_W
COL_QD = COL_QA + ATT_Q_W
COL_KD = COL_QD + DN_WIDTH
COL_VD = COL_KD + DN_WIDTH
COL_ZD = COL_VD + DN_WIDTH
COL_FT = COL_ZD + DN_WIDTH
COL_POOL = COL_FT + BRANCH_WIDTH
COL_KA = COL_POOL + BRANCH_WIDTH
COL_VA = COL_KA + ATT_KV_W
COL_AB = COL_VA + ATT_KV_W
COLS_PAD = 24576


def _params(*sem):
    return pltpu.CompilerParams(dimension_semantics=sem, vmem_limit_bytes=VMEM_LIMIT_BYTES)


def _silu(x):
    return x * jax.nn.sigmoid(x)


def _ada_kernel(c_ref, w_ref, b_ref, o_ref):
    s = _silu(c_ref[...]).astype(BF16)
    o_ref[...] = jnp.dot(s, w_ref[...].astype(BF16), preferred_element_type=F32) + b_ref[...]


def ada_modulation(cond, ada_w, ada_b, tn=512):
    depth, d, n = ada_w.shape
    rows = cond.shape[0]
    return pl.pallas_call(
        _ada_kernel,
        out_shape=jax.ShapeDtypeStruct((depth, rows, n), F32),
        grid=(depth, n // tn),
        in_specs=[
            pl.BlockSpec((rows, d), lambda l, j: (0, 0)),
            pl.BlockSpec((None, d, tn), lambda l, j: (l, 0, j)),
            pl.BlockSpec((None, 1, tn), lambda l, j: (l, 0, j)),
        ],
        out_specs=pl.BlockSpec((None, rows, tn), lambda l, j: (l, 0, j)),
        compiler_params=_params("parallel", "parallel"),
        name="ada_modulation",
    )(cond, ada_w, ada_b.reshape(depth, 1, n))


def _modulated_norm(x, g, sc1p, sh):
    y = x * lax.rsqrt(jnp.mean(x * x, axis=-1, keepdims=True) + EPS) * g
    return y * sc1p + sh


def _norm_kernel(x_ref, g_ref, sc_ref, sh_ref, o_ref):
    o_ref[...] = _modulated_norm(x_ref[...], g_ref[...], sc_ref[...], sh_ref[...]).astype(o_ref.dtype)


def _matmul_kernel(a_ref, w_ref, o_ref):
    o_ref[...] = jnp.dot(a_ref[...], w_ref[...], preferred_element_type=F32).astype(o_ref.dtype)


def norm_project(x2d, norm_w, sc1p, sh, w, rows_per_mod, tm=1024, tn=1024, out_dtype=F32):
    m, d = x2d.shape
    n = w.shape[1]
    tr = min(256, m)
    mod_idx = lambda i: ((i * tr) // rows_per_mod, 0, 0)
    hn = pl.pallas_call(
        _norm_kernel,
        out_shape=jax.ShapeDtypeStruct((m, d), BF16),
        grid=(m // tr,),
        in_specs=[
            pl.BlockSpec((tr, d), lambda i: (i, 0)),
            pl.BlockSpec((1, d), lambda i: (0, 0)),
            pl.BlockSpec((None, 1, d), mod_idx),
            pl.BlockSpec((None, 1, d), mod_idx),
        ],
        out_specs=pl.BlockSpec((tr, d), lambda i: (i, 0)),
        compiler_params=_params("parallel"),
        name="modulated_norm",
    )(x2d, norm_w.reshape(1, d), sc1p, sh)
    tm = min(tm, m)
    return pl.pallas_call(
        _matmul_kernel,
        out_shape=jax.ShapeDtypeStruct((m, n), out_dtype),
        grid=(m // tm, n // tn),
        in_specs=[
            pl.BlockSpec((tm, d), lambda i, j: (i, 0)),
            pl.BlockSpec((d, tn), lambda i, j: (0, j)),
        ],
        out_specs=pl.BlockSpec((tm, tn), lambda i, j: (i, j)),
        compiler_params=_params("parallel", "arbitrary"),
        name="project",
    )(hn, w)


def _rope(x, cos, sin_signed):
    lane = lax.broadcasted_iota(jnp.int32, x.shape, 1)
    swapped = jnp.where((lane % 64) < 32, pltpu.roll(x, 96, 1), pltpu.roll(x, 32, 1))
    return x * cos + swapped * sin_signed


def _attn_kernel(*refs, banded, n_ctx, seq_len):
    if banded:
        (sink_ref, q_ref, kp_ref, kc_ref, kn_ref, vp_ref, vc_ref, vn_ref, kx_ref, vx_ref,
         cos_ref, sin_ref, o_ref) = refs
    else:
        sink_ref, q_ref, kx_ref, vx_ref, o_ref = refs
    blk = q_ref.shape[0]
    scale = HEAD_DIM ** -0.5
    n = pl.program_id(1)
    if banded:
        def table(ref, blk_idx):
            start = pl.multiple_of(blk_idx * blk, blk)
            return ref[pl.ds(start, blk), :]
        nblocks = seq_len // blk
        ip = jnp.maximum(n - 1, 0)
        inx = jnp.minimum(n + 1, nblocks - 1)
        cos_q, sin_q = table(cos_ref, n), table(sin_ref, n)
        cos_p, sin_p = table(cos_ref, ip), table(sin_ref, ip)
        cos_n, sin_n = table(cos_ref, inx), table(sin_ref, inx)
        rows = lax.broadcasted_iota(jnp.int32, (ATT_GROUP * blk, 3 * blk), 0) % blk
        cols = lax.broadcasted_iota(jnp.int32, (ATT_GROUP * blk, 3 * blk), 1)
        rel = cols - blk - rows
        kpos = (n - 1) * blk + cols
        valid = (jnp.abs(rel) <= WINDOW) & (kpos >= 0) & (kpos < seq_len)
    for h in range(ATT_KV_HEADS):
        hs = slice(h * HEAD_DIM, (h + 1) * HEAD_DIM)
        qs, sinks = [], []
        for g in range(ATT_GROUP):
            hq = h * ATT_GROUP + g
            qh = q_ref[:, hq * HEAD_DIM:(hq + 1) * HEAD_DIM]
            if banded:
                qh = _rope(qh, cos_q, sin_q)
            qs.append((qh * scale).astype(BF16))
            sinks.append(jnp.full((blk, 1), sink_ref[hq], F32))
        q4 = jnp.concatenate(qs, axis=0)
        sink = jnp.concatenate(sinks, axis=0)
        kx = kx_ref[:, hs].astype(BF16)
        vx = vx_ref[:, hs].astype(BF16)
        s_ctx = lax.dot_general(q4, kx, (((1,), (1,)), ((), ())), preferred_element_type=F32)
        m = jnp.maximum(jnp.max(s_ctx, axis=-1, keepdims=True), sink)
        if banded:
            kband = jnp.concatenate([
                _rope(kp_ref[:, hs], cos_p, sin_p), _rope(kc_ref[:, hs], cos_q, sin_q),
                _rope(kn_ref[:, hs], cos_n, sin_n)], axis=0).astype(BF16)
            vband = jnp.concatenate([vp_ref[:, hs], vc_ref[:, hs], vn_ref[:, hs]], axis=0).astype(BF16)
            s_loc = lax.dot_general(q4, kband, (((1,), (1,)), ((), ())), preferred_element_type=F32)
            s_loc = jnp.where(valid, s_loc, NEG_INF)
            m = jnp.maximum(m, jnp.max(s_loc, axis=-1, keepdims=True))
        p_ctx = jnp.exp(s_ctx - m)
        denom = jnp.sum(p_ctx, axis=-1, keepdims=True) + jnp.exp(sink - m)
        o = jnp.dot(p_ctx.astype(BF16), vx, preferred_element_type=F32)
        if banded:
            p_loc = jnp.exp(s_loc - m)
            denom = denom + jnp.sum(p_loc, axis=-1, keepdims=True)
            o = o + jnp.dot(p_loc.astype(BF16), vband, preferred_element_type=F32)
        o = o / denom
        for g in range(ATT_GROUP):
            hq = h * ATT_GROUP + g
            o_ref[:, hq * HEAD_DIM:(hq + 1) * HEAD_DIM] = o[g * blk:(g + 1) * blk].astype(o_ref.dtype)


def latent_attention(p_lat, p_ctx, sink, cos_t, sin_t, batch, seq_len, n_ctx, ctx_col0=0):
    blk = ATT_BLOCK
    nb = seq_len // blk
    cq = COL_QA // ATT_Q_W
    ck, cv = COL_KA // ATT_KV_W, COL_VA // ATT_KV_W
    xk, xv = (COL_KA - ctx_col0) // ATT_KV_W, (COL_VA - ctx_col0) // ATT_KV_W
    row = lambda b, n: b * nb + n
    kspec = lambda col, shift: pl.BlockSpec(
        (blk, ATT_KV_W), lambda b, n: (row(b, jnp.clip(n + shift, 0, nb - 1)), col))
    return pl.pallas_call(
        functools.partial(_attn_kernel, banded=True, n_ctx=n_ctx, seq_len=seq_len),
        out_shape=jax.ShapeDtypeStruct((batch * seq_len, ATT_Q_W), BF16),
        grid=(batch, nb),
        in_specs=[
            pl.BlockSpec(memory_space=pltpu.SMEM),
            pl.BlockSpec((blk, ATT_Q_W), lambda b, n: (row(b, n), cq)),
            kspec(ck, -1), kspec(ck, 0), kspec(ck, 1),
            kspec(cv, -1), kspec(cv, 0), kspec(cv, 1),
            pl.BlockSpec((n_ctx, ATT_KV_W), lambda b, n: (b, xk)),
            pl.BlockSpec((n_ctx, ATT_KV_W), lambda b, n: (b, xv)),
            pl.BlockSpec((seq_len, HEAD_DIM), lambda b, n: (0, 0)),
            pl.BlockSpec((seq_len, HEAD_DIM), lambda b, n: (0, 0)),
        ],
        out_specs=pl.BlockSpec((blk, ATT_Q_W), lambda b, n: (row(b, n), 0)),
        compiler_params=_params("parallel", "arbitrary"),
        name="latent_attention",
    )(sink, p_lat, p_lat, p_lat, p_lat, p_lat, p_lat, p_lat, p_ctx, p_ctx, cos_t, sin_t)


def context_attention(p_ctx, sink, batch, n_ctx):
    cq = COL_QA // ATT_Q_W
    ck, cv = COL_KA // ATT_KV_W, COL_VA // ATT_KV_W
    return pl.pallas_call(
        functools.partial(_attn_kernel, banded=False, n_ctx=n_ctx, seq_len=n_ctx),
        out_shape=jax.ShapeDtypeStruct((batch * n_ctx, ATT_Q_W), BF16),
        grid=(batch, 1),
        in_specs=[
            pl.BlockSpec(memory_space=pltpu.SMEM),
            pl.BlockSpec((n_ctx, ATT_Q_W), lambda b, n: (b, cq)),
            pl.BlockSpec((n_ctx, ATT_KV_W), lambda b, n: (b, ck)),
            pl.BlockSpec((n_ctx, ATT_KV_W), lambda b, n: (b, cv)),
        ],
        out_specs=pl.BlockSpec((n_ctx, ATT_Q_W), lambda b, n: (b, 0)),
        compiler_params=_params("parallel", "arbitrary"),
        name="context_attention",
    )(sink, p_ctx, p_ctx, p_ctx)


def rope_tables(seq_len):
    rows = seq_len // GRID_W
    row = jnp.repeat(jnp.arange(rows, dtype=F32), GRID_W)
    col = jnp.tile(jnp.arange(GRID_W, dtype=F32), rows)
    half = HEAD_DIM // 2
    inv_freq = ROPE_BASE ** (-jnp.arange(0, half, 2, dtype=F32) / half)
    ang_r, ang_c = row[:, None] * inv_freq, col[:, None] * inv_freq
    cos_t = jnp.concatenate([jnp.cos(ang_r)] * 2 + [jnp.cos(ang_c)] * 2, axis=-1)
    sin_t = jnp.concatenate([-jnp.sin(ang_r), jnp.sin(ang_r), -jnp.sin(ang_c), jnp.sin(ang_c)], axis=-1)
    return cos_t, sin_t


def _fourier_kernel(u_ref, cc_ref, sc_ref, ct_ref, st_ref, o_ref, p_ref, q_ref, *, scale):
    @pl.when(pl.program_id(2) == 0)
    def _():
        u = u_ref[...].astype(BF16)
        p_ref[...] = jnp.dot(u, cc_ref[...], preferred_element_type=F32).astype(BF16)
        q_ref[...] = jnp.dot(u, sc_ref[...], preferred_element_type=F32).astype(BF16)

    re = (jnp.dot(ct_ref[...], p_ref[...], preferred_element_type=F32)
          - jnp.dot(st_ref[...], q_ref[...], preferred_element_type=F32))
    o_ref[...] = (re * scale).astype(o_ref.dtype)


def dft_tables(n):
    k = jnp.arange(n, dtype=jnp.int32)
    ang = ((k[:, None] * k[None, :]) % n).astype(F32) * (2.0 * math.pi / n)
    return jnp.cos(ang).astype(BF16), jnp.sin(ang).astype(BF16)


def fourier_mix(p2d, batch, seq_len, tr=512):
    tr = min(tr, seq_len)
    cc, sc = dft_tables(FT_GROUP_DIM)
    ct, st = dft_tables(seq_len)
    c0 = COL_FT // FT_GROUP_DIM
    nr = seq_len // tr
    return pl.pallas_call(
        functools.partial(_fourier_kernel, scale=1.0 / math.sqrt(seq_len * FT_GROUP_DIM)),
        out_shape=jax.ShapeDtypeStruct((batch * seq_len, BRANCH_WIDTH), BF16),
        grid=(batch, FT_GROUPS, nr),
        in_specs=[
            pl.BlockSpec((seq_len, FT_GROUP_DIM), lambda b, g, r: (b, c0 + g)),
            pl.BlockSpec((FT_GROUP_DIM, FT_GROUP_DIM), lambda b, g, r: (0, 0)),
            pl.BlockSpec((FT_GROUP_DIM, FT_GROUP_DIM), lambda b, g, r: (0, 0)),
            pl.BlockSpec((tr, seq_len), lambda b, g, r: (r, 0)),
            pl.BlockSpec((tr, seq_len), lambda b, g, r: (r, 0)),
        ],
        out_specs=pl.BlockSpec((tr, FT_GROUP_DIM), lambda b, g, r: (b * nr + r, g)),
        scratch_shapes=[pltpu.VMEM((seq_len, FT_GROUP_DIM), BF16)] * 2,
        compiler_params=_params("parallel", "parallel", "arbitrary"),
        name="fourier_mix",
    )(p2d, cc, sc, ct, st)


POOL_TILE = 256
POOL_HALO = 128


def _pool_kernel(u_ref, band_ref, w_ref, s_ref, o_ref, pad_ref, *, seq_len):
    g = pl.program_id(1)
    zeros = jnp.zeros((POOL_HALO, POOL_GROUP_DIM), BF16)
    pad_ref[0:POOL_HALO, :] = zeros
    pad_ref[POOL_HALO + seq_len:POOL_HALO + seq_len + POOL_HALO, :] = zeros
    pad_ref[POOL_HALO:POOL_HALO + seq_len, :] = u_ref[...].astype(BF16)
    half = jnp.left_shift(1, g)
    for t in range(seq_len // POOL_TILE):
        r0 = t * POOL_TILE
        win = pad_ref[r0:r0 + POOL_TILE + 2 * POOL_HALO, :]
        sums = jnp.dot(band_ref[...], win, preferred_element_type=F32)
        pos = r0 + lax.broadcasted_iota(jnp.int32, (POOL_TILE, 1), 0)
        cnt = jnp.minimum(pos + half, seq_len) - jnp.maximum(pos - half, 0)
        m = sums / cnt.astype(F32) - u_ref[r0:r0 + POOL_TILE, :]
        y = jnp.dot(m.astype(BF16), w_ref[...], preferred_element_type=F32) * s_ref[...]
        o_ref[r0:r0 + POOL_TILE, :] = y.astype(o_ref.dtype)


def pool_bands():
    r = np.arange(POOL_TILE)[:, None]
    c = np.arange(POOL_TILE + 2 * POOL_HALO)[None, :]
    rel = c - POOL_HALO - r
    bands = [((rel >= -(w // 2)) & (rel <= w // 2 - 1)) for w in POOL_WINDOWS]
    return jnp.asarray(np.stack(bands).astype(np.float32), dtype=BF16)


def pool_mix(p2d, pool_w, pool_scale, batch, seq_len):
    c0 = COL_POOL // POOL_GROUP_DIM
    ng = len(POOL_WINDOWS)
    return pl.pallas_call(
        functools.partial(_pool_kernel, seq_len=seq_len),
        out_shape=jax.ShapeDtypeStruct((batch * seq_len, BRANCH_WIDTH), BF16),
        grid=(batch, ng),
        in_specs=[
            pl.BlockSpec((seq_len, POOL_GROUP_DIM), lambda b, g: (b, c0 + g)),
            pl.BlockSpec((None, POOL_TILE, POOL_TILE + 2 * POOL_HALO), lambda b, g: (g, 0, 0)),
            pl.BlockSpec((None, POOL_GROUP_DIM, POOL_GROUP_DIM), lambda b, g: (g, 0, 0)),
            pl.BlockSpec((1, POOL_GROUP_DIM), lambda b, g: (0, g)),
        ],
        out_specs=pl.BlockSpec((seq_len, POOL_GROUP_DIM), lambda b, g: (b, g)),
        scratch_shapes=[pltpu.VMEM((seq_len + 2 * POOL_HALO, POOL_GROUP_DIM), BF16)],
        compiler_params=_params("parallel", "parallel"),
        name="pool_mix",
    )(p2d, pool_bands(), pool_w.astype(BF16), pool_scale.reshape(1, BRANCH_WIDTH))


def permute_projection_columns(w):
    o_ka, o_va, o_kd, o_vd, o_a, o_b = 0, 256, 512, 1536, 2560, 2576
    o_qa, o_qd, o_zd, o_ft, o_pool, o_gate = 2592, 3616, 4640, 5664, 6688, 7712
    seg = lambda o, n: w[..., o:o + n]
    parts = [seg(o_gate, GATE_W), seg(o_qa, ATT_Q_W), seg(o_qd, DN_WIDTH), seg(o_kd, DN_WIDTH),
             seg(o_vd, DN_WIDTH), seg(o_zd, DN_WIDTH), seg(o_ft, BRANCH_WIDTH), seg(o_pool, BRANCH_WIDTH),
             seg(o_ka, ATT_KV_W), seg(o_va, ATT_KV_W), seg(o_a, 2 * DN_HEADS), seg(o_b, 2 * DN_HEADS)]
    used = sum(p.shape[-1] for p in parts)
    parts.append(jnp.zeros(w.shape[:-1] + (COLS_PAD - used,), w.dtype))
    return jnp.concatenate(parts, axis=-1)


CONV_PAD = 8
N_LEVELS = 6


def dn_masks():
    i = np.arange(DN_SUPER)[:, None]
    j = np.arange(DN_SUPER)[None, :]
    incl, lev = [], []
    for later in (lambda a, b: a >= b, lambda a, b: a <= b):
        strict = later(i, j) & (i != j)
        incl.append((i // DN_CHUNK == j // DN_CHUNK) & later(i, j))
        lv = [(i // 2 == j // 2) & strict]
        sz = 2
        while sz < DN_CHUNK:
            lv.append((i // (2 * sz) == j // (2 * sz)) & (i // sz != j // sz) & strict)
            sz *= 2
        lev.append(np.stack(lv))
    return (jnp.asarray(np.stack(incl).astype(np.float32)), jnp.asarray(np.stack(lev).astype(np.float32)))


def _split3(x):
    x1 = x.astype(BF16)
    r1 = x - x1.astype(F32)
    x2 = r1.astype(BF16)
    x3 = (r1 - x2.astype(F32)).astype(BF16)
    return x1, x2, x3


def _softplus(x):
    return jnp.maximum(x, 0.0) + jnp.log1p(jnp.exp(-jnp.abs(x)))


def _dn_kernel(alog_ref, dtb_ref, q_ref, k_ref, v_ref, z_ref, ab_ref, cq_ref, ck_ref, cv_ref, nw_ref,
               s0_ref, incl_ref, lev_ref, o_ref, sfin_ref,
               pq_ref, pk_ref, pv_ref, qe_ref, g_ref, b_ref, eg_ref, oacc_ref, *, seq_len, with_q):
    h = pl.program_id(1)
    n_super = seq_len // DN_SUPER
    n_chunks = seq_len // DN_CHUNK
    per_super = DN_SUPER // DN_CHUNK
    zpad = jnp.zeros((CONV_PAD, HEAD_DIM), F32)
    streams = [(pk_ref, k_ref), (pv_ref, v_ref)] + ([(pq_ref, q_ref)] if with_q else [])
    for pad_ref, src_ref in streams:
        pad_ref[0:CONV_PAD, :] = zpad
        pad_ref[CONV_PAD + seq_len:2 * CONV_PAD + seq_len, :] = zpad
        pad_ref[CONV_PAD:CONV_PAD + seq_len, :] = src_ref[...]

    lane = lax.broadcasted_iota(jnp.int32, (1, HEAD_DIM), 1)
    pick = lambda ref: jnp.where(lane == 0, ref[0, h], jnp.where(lane == 1, ref[1, h], 0.0))
    neg_a = -jnp.exp(pick(alog_ref))
    dtb = pick(dtb_ref)
    eye = (lax.broadcasted_iota(jnp.int32, (DN_SUPER, DN_SUPER), 0)
           == lax.broadcasted_iota(jnp.int32, (DN_SUPER, DN_SUPER), 1)).astype(F32)
    half = DN_CONV // 2

    def conv_silu(pad_ref, cw_ref, r0):
        win = pad_ref[pl.ds(r0, DN_SUPER + 2 * CONV_PAD), :]
        acc = None
        for j in range(DN_CONV):
            off = CONV_PAD - half + j
            term = cw_ref[j:j + 1, :] * win[off:off + DN_SUPER, :]
            acc = term if acc is None else acc + term
        return _silu(acc)

    def l2n(x):
        return x * lax.rsqrt(jnp.sum(x * x, axis=-1, keepdims=True) + EPS)

    def chain_setup(sidx):
        r0 = pl.multiple_of(sidx * DN_SUPER, DN_SUPER)
        k = l2n(conv_silu(pk_ref, ck_ref, r0))
        v = conv_silu(pv_ref, cv_ref, r0)
        kb = k.astype(BF16)
        kk = lax.dot_general(kb, kb, (((1,), (1,)), ((), ())), preferred_element_type=F32)
        q = qk = None
        if with_q:
            q = l2n(conv_silu(pq_ref, cq_ref, r0)) * (HEAD_DIM ** -0.5)
            qk = lax.dot_general(q.astype(BF16), kb, (((1,), (1,)), ((), ())), preferred_element_type=F32)
        ab = ab_ref[pl.ds(r0, DN_SUPER), :]
        gt = neg_a * _softplus(ab + dtb)
        bt = jax.nn.sigmoid(ab)
        gparts = _split3(gt)
        gcs = []
        for d in range(2):
            seg = incl_ref[d].astype(BF16)
            gcs.append(sum(jnp.dot(seg, gp, preferred_element_type=F32) for gp in gparts))
        gtot_t = gcs[0] + gcs[1] - gt
        chains = []
        for d in range(2):
            gcol = gcs[d][:, d:d + 1]
            grow = jnp.transpose(gcs[d])[d:d + 1, :]
            beta = bt[:, 2 + d:3 + d]
            e = jnp.exp((gcol - grow) * incl_ref[d])
            nf = (beta * kk) * e
            chains.append(dict(d=d, sidx=sidx, r0=r0, k=k, v=v, q=q, qk=qk, gcol=gcol, gtot=gtot_t[:, d:d + 1],
                               beta=beta, e=e, nf=nf, t=eye - nf * lev_ref[d, 0]))
        return chains

    def chain_finish(ch):
        d, k, gcol, gtot, beta = ch["d"], ch["k"], ch["gcol"], ch["gtot"], ch["beta"]
        egc = jnp.exp(gcol)
        rhs = jnp.concatenate([beta * ch["v"], (beta * egc) * k], axis=1).astype(BF16)
        sol = jnp.dot(ch["t"].astype(BF16), rhs, preferred_element_type=F32)
        u, w = sol[:, :HEAD_DIM], sol[:, HEAD_DIM:]
        wu = jnp.concatenate([w, u], axis=1).astype(BF16)
        kd = (k * jnp.exp(gtot - gcol)).astype(BF16)
        if with_q:
            qkm = (ch["qk"] * ch["e"] * incl_ref[d]).astype(BF16)
            r2 = jnp.dot(qkm, wu, preferred_element_type=F32)
            rows = pl.ds(ch["r0"], DN_SUPER)
            qe_ref[d, rows, :] = (ch["q"] * egc - r2[:, :HEAD_DIM]).astype(BF16)
            oacc_ref[d, rows, :] = r2[:, HEAD_DIM:]
        eg = jnp.exp(gtot)
        for c in range(per_super):
            cs = slice(c * DN_CHUNK, (c + 1) * DN_CHUNK)
            gb = lax.dot_general(kd[cs], wu[cs], (((0,), (0,)), ((), ())), preferred_element_type=F32)
            chunk = ch["sidx"] * per_super + c
            srows = pl.ds(pl.multiple_of(chunk * HEAD_DIM, HEAD_DIM), HEAD_DIM)
            g_ref[d, srows, :] = gb[:, :HEAD_DIM].astype(BF16)
            b_ref[d, srows, :] = gb[:, HEAD_DIM:]
            blk = jnp.broadcast_to(eg[c * DN_CHUNK:c * DN_CHUNK + 8, :], (8, HEAD_DIM))
            eg_ref[d, pl.ds(pl.multiple_of(chunk * 8, 8), 8), :] = blk

    group = 2 if n_super % 2 == 0 else 1

    def super_group(gi, carry):
        chains = [ch for s in range(group) for ch in chain_setup(gi * group + s)]
        for m in range(1, N_LEVELS):
            for ch in chains:
                ch["tb"] = ch["t"].astype(BF16)
                ch["x"] = jnp.dot((ch["nf"] * lev_ref[ch["d"], m]).astype(BF16), ch["tb"],
                                  preferred_element_type=F32)
            for ch in chains:
                ch["t"] = ch["t"] - jnp.dot(ch["tb"], ch["x"].astype(BF16), preferred_element_type=F32)
        for ch in chains:
            chain_finish(ch)
        return carry

    lax.fori_loop(0, n_super // group, super_group, 0)

    def chunk_step(c, states):
        ccs = (c, n_chunks - 1 - c)
        prods = []
        for d in range(2):
            srows = pl.ds(pl.multiple_of(ccs[d] * HEAD_DIM, HEAD_DIM), HEAD_DIM)
            lhs = g_ref[d, srows, :]
            if with_q:
                rows = pl.ds(pl.multiple_of(ccs[d] * DN_CHUNK, DN_CHUNK), DN_CHUNK)
                lhs = jnp.concatenate([lhs, qe_ref[d, rows, :]], axis=0)
            prods.append(jnp.dot(lhs, states[d].astype(BF16), preferred_element_type=F32))
        new_states = []
        for d in range(2):
            srows = pl.ds(pl.multiple_of(ccs[d] * HEAD_DIM, HEAD_DIM), HEAD_DIM)
            decay = eg_ref[d, pl.ds(pl.multiple_of(ccs[d] * 8, 8), 8), :][0:1, :]
            new_states.append(decay * states[d] - prods[d][:HEAD_DIM] + b_ref[d, srows, :])
            if with_q:
                rows = pl.ds(pl.multiple_of(ccs[d] * DN_CHUNK, DN_CHUNK), DN_CHUNK)
                oacc_ref[d, rows, :] += prods[d][HEAD_DIM:]
        return tuple(new_states)

    s_f, s_b = lax.fori_loop(0, n_chunks, chunk_step, (s0_ref[0], s0_ref[1]))
    sfin_ref[0] = s_f
    sfin_ref[1] = s_b
    if with_q:
        o = oacc_ref[0] + oacc_ref[1]
        on = o * lax.rsqrt(jnp.mean(o * o, axis=-1, keepdims=True) + EPS) * nw_ref[...]
        o_ref[...] = (on * _silu(z_ref[...])).astype(o_ref.dtype)
    else:
        o_ref[...] = jnp.zeros(o_ref.shape, o_ref.dtype)


def head_gate_layout(p2d, col0=0):
    m = p2d.shape[0]
    c = COL_AB - col0
    ab = p2d[:, c:c + 4 * DN_HEADS].astype(F32).reshape(m, 4, DN_HEADS)
    ab = jnp.transpose(ab, (0, 2, 1))
    ab = jnp.pad(ab, ((0, 0), (0, 0), (0, LANES - 4)))
    return ab.reshape(m, DN_HEADS * LANES)


def deltanet_mix(p2d, a_log, dt_bias, conv_w, norm_w, s0, batch, seq_len, with_q=True, col0=0):
    incl, lev = dn_masks()
    ab = head_gate_layout(p2d, col0)
    cols = (COL_QD, COL_KD, COL_VD, COL_ZD) if with_q else (COL_KD, COL_KD, COL_VD, COL_KD)
    cqb, ckb, cvb, czb = ((c - col0) // HEAD_DIM for c in cols)
    seq_spec = lambda c0: pl.BlockSpec((seq_len, HEAD_DIM), lambda b, h: (b, c0 + h))
    conv_spec = lambda c0: pl.BlockSpec((DN_CONV, HEAD_DIM), lambda b, h: (0, c0 + h))
    state_spec = pl.BlockSpec((None, None, 2, HEAD_DIM, HEAD_DIM), lambda b, h: (b, h, 0, 0, 0))
    smem = pl.BlockSpec(memory_space=pltpu.SMEM)
    n_chunks = seq_len // DN_CHUNK
    per_dir = lambda dt: pltpu.VMEM((2, seq_len, HEAD_DIM), dt)
    pad = pltpu.VMEM((seq_len + 2 * CONV_PAD, HEAD_DIM), F32)
    return pl.pallas_call(
        functools.partial(_dn_kernel, seq_len=seq_len, with_q=with_q),
        out_shape=(jax.ShapeDtypeStruct((batch * seq_len, DN_WIDTH), BF16),
                   jax.ShapeDtypeStruct((batch, DN_HEADS, 2, HEAD_DIM, HEAD_DIM), F32)),
        grid=(batch, DN_HEADS),
        in_specs=[
            smem, smem,
            seq_spec(cqb), seq_spec(ckb), seq_spec(cvb), seq_spec(czb),
            pl.BlockSpec((seq_len, LANES), lambda b, h: (b, h)),
            conv_spec(0), conv_spec(DN_HEADS), conv_spec(2 * DN_HEADS),
            pl.BlockSpec((1, HEAD_DIM), lambda b, h: (0, 0)),
            state_spec,
            pl.BlockSpec((2, DN_SUPER, DN_SUPER), lambda b, h: (0, 0, 0)),
            pl.BlockSpec((2, N_LEVELS, DN_SUPER, DN_SUPER), lambda b, h: (0, 0, 0, 0)),
        ],
        out_specs=(pl.BlockSpec((seq_len, HEAD_DIM), lambda b, h: (b, h)), state_spec),
        scratch_shapes=[pad, pad, pad, per_dir(BF16),
                        pltpu.VMEM((2, n_chunks * HEAD_DIM, HEAD_DIM), BF16),
                        pltpu.VMEM((2, n_chunks * HEAD_DIM, HEAD_DIM), F32),
                        pltpu.VMEM((2, n_chunks * 8, HEAD_DIM), F32), per_dir(F32)],
        compiler_params=_params("parallel", "parallel"),
        name="deltanet_mix",
    )(a_log, dt_bias, p2d, p2d, p2d, p2d, ab, conv_w, conv_w, conv_w, norm_w.reshape(1, HEAD_DIM),
      s0, incl, lev)


def _merge_gate_kernel(y0, y1, y2, y3, g0, g1, g2, g3, wb_ref, o_ref):
    acc = None
    for n, (y_ref, g_ref) in enumerate(((y0, g0), (y1, g1), (y2, g2), (y3, g3))):
        t = jnp.dot(y_ref[...], wb_ref[n], preferred_element_type=F32) * jax.nn.sigmoid(g_ref[...].astype(F32))
        acc = t if acc is None else acc + t
    o_ref[...] = acc.astype(o_ref.dtype)


def merge_gate(branches, p2d, wb, tm=512, tn=1024):
    m = p2d.shape[0]
    d = wb.shape[-1]
    tm = min(tm, m)
    nj = d // tn
    y_spec = pl.BlockSpec((tm, BRANCH_WIDTH), lambda i, j: (i, 0))
    g_spec = lambda n: pl.BlockSpec((tm, tn), lambda i, j: (i, COL_GATE // tn + n * nj + j))
    return pl.pallas_call(
        _merge_gate_kernel,
        out_shape=jax.ShapeDtypeStruct((m, d), BF16),
        grid=(m // tm, nj),
        in_specs=[y_spec] * N_BRANCHES + [g_spec(n) for n in range(N_BRANCHES)]
        + [pl.BlockSpec((N_BRANCHES, BRANCH_WIDTH, tn), lambda i, j: (0, 0, j))],
        out_specs=pl.BlockSpec((tm, tn), lambda i, j: (i, j)),
        compiler_params=_params("parallel", "arbitrary"),
        name="merge_gate",
    )(*branches, p2d, p2d, p2d, p2d, wb)


def _out_proj_kernel(a_ref, w_ref, x_ref, g_ref, o_ref):
    o_ref[...] = x_ref[...] + g_ref[...] * jnp.dot(a_ref[...], w_ref[...], preferred_element_type=F32)


def out_project_residual(a, w, x2d, gate, rows_per_mod, tm=512, tn=1024):
    m, d = x2d.shape
    tm = min(tm, m)
    return pl.pallas_call(
        _out_proj_kernel,
        out_shape=jax.ShapeDtypeStruct((m, d), F32),
        grid=(m // tm, d // tn),
        in_specs=[
            pl.BlockSpec((tm, a.shape[1]), lambda i, j: (i, 0)),
            pl.BlockSpec((a.shape[1], tn), lambda i, j: (0, j)),
            pl.BlockSpec((tm, tn), lambda i, j: (i, j)),
            pl.BlockSpec((None, 1, tn), lambda i, j: ((i * tm) // rows_per_mod, 0, j)),
        ],
        out_specs=pl.BlockSpec((tm, tn), lambda i, j: (i, j)),
        compiler_params=_params("parallel", "arbitrary"),
        name="out_project_residual",
    )(a, w, x2d, gate)


def _first_argmax(v, lane):
    m = jnp.max(v, axis=-1, keepdims=True)
    idx = jnp.min(jnp.where(v == m, lane, LANES), axis=-1, keepdims=True)
    return m, idx


HI_MASK = 0xFFFF0000


def _pack_halves(v):
    half = v.shape[1] // 2
    lo = pltpu.bitcast(v[:, :half].astype(BF16).astype(F32), jnp.uint32)
    hi = pltpu.bitcast(v[:, half:].astype(BF16).astype(F32), jnp.uint32)
    return hi | (lo >> 16)


def _unpack_halves(p):
    return pltpu.bitcast(p << 16, F32), pltpu.bitcast(p & jnp.uint32(HI_MASK), F32)


ROW_CHUNKS = 16


def _store_row_blocked(ref, packed):
    rows = packed.shape[0]
    for c in range(ROW_CHUNKS):
        ref[pl.ds(c, rows, stride=ROW_CHUNKS), :] = packed[:, c * LANES:(c + 1) * LANES]


def _load_row_blocked(ref, rows):
    return jnp.concatenate([ref[pl.ds(c, rows, stride=ROW_CHUNKS), :] for c in range(ROW_CHUNKS)], axis=1)


def _route_kernel(x_ref, g_ref, sc_ref, sh_ref, rw_ref, rb_ref, tri_ref, cnt0_ref,
                  h_ref, ids_ref, wts_ref, rank_ref, cnt_ref, run_ref):
    h = _modulated_norm(x_ref[...], g_ref[...], sc_ref[...], sh_ref[...])
    _store_row_blocked(h_ref, _pack_halves(h))
    logits = jnp.dot(h, rw_ref[...], preferred_element_type=F32, precision=lax.Precision.HIGHEST)
    scores = jax.nn.sigmoid(logits)
    lane = lax.broadcasted_iota(jnp.int32, scores.shape, 1)
    grp = lane // (N_EXPERTS // N_EXPERT_GROUPS)
    ninf = -jnp.inf
    vb = jnp.where(lane < N_EXPERTS, scores + rb_ref[...], ninf)
    gs = jnp.full(scores.shape, ninf, F32)
    for g in range(N_EXPERT_GROUPS):
        vg = jnp.where(grp == g, vb, ninf)
        m1, i1 = _first_argmax(vg, lane)
        m2 = jnp.max(jnp.where(lane == i1, ninf, vg), axis=-1, keepdims=True)
        gs = jnp.where(lane == g, m1 + m2, gs)
    allowed = jnp.zeros(scores.shape, F32)
    for _ in range(TOPK_GROUPS):
        _, gi = _first_argmax(gs, lane)
        gs = jnp.where(lane == gi, ninf, gs)
        allowed = jnp.where(grp == gi, 1.0, allowed)
    ve = jnp.where(allowed > 0, vb, ninf)
    ids = jnp.zeros(scores.shape, jnp.int32)
    wts = jnp.zeros(scores.shape, F32)
    sel = jnp.zeros(scores.shape, F32)
    hits = []
    for k in range(TOP_K):
        _, ei = _first_argmax(ve, lane)
        hit = lane == ei
        hits.append(hit)
        sk = jnp.sum(jnp.where(hit, scores, 0.0), axis=-1, keepdims=True)
        ids = jnp.where(lane == k, ei, ids)
        wts = jnp.where(lane == k, sk, wts)
        sel = jnp.where(hit, 1.0, sel)
        ve = jnp.where(hit, ninf, ve)
    ids_ref[...] = ids
    wts_ref[...] = wts / jnp.sum(wts, axis=-1, keepdims=True) * ROUTE_SCALE

    @pl.when(pl.program_id(0) == 0)
    def _():
        run_ref[...] = cnt0_ref[...]

    before = jnp.dot(tri_ref[...], sel.astype(BF16), preferred_element_type=F32) + run_ref[...]
    ranks = jnp.zeros(scores.shape, F32)
    for k in range(TOP_K):
        rk = jnp.sum(jnp.where(hits[k], before, 0.0), axis=-1, keepdims=True)
        ranks = jnp.where(lane == k, rk, ranks)
    rank_ref[...] = ranks.astype(jnp.int32)
    run_ref[...] += jnp.sum(sel, axis=0, keepdims=True)
    cnt_ref[...] = run_ref[...]


def route(x2d, norm_w, sc1p, sh, router_w, router_bias, rows_per_mod, cnt0, tm=256):
    m, d = x2d.shape
    tm = min(tm, m)
    rw = jnp.pad(router_w, ((0, 0), (0, LANES - N_EXPERTS)))
    rb = jnp.pad(router_bias, (0, LANES - N_EXPERTS)).reshape(1, LANES)
    tri = jnp.asarray(np.tril(np.ones((tm, tm), np.float32), -1), dtype=BF16)
    mod_idx = lambda i: ((i * tm) // rows_per_mod, 0, 0)
    tok_lanes = pl.BlockSpec((tm, LANES), lambda i: (i, 0))
    one_row = pl.BlockSpec((1, LANES), lambda i: (0, 0))
    return pl.pallas_call(
        _route_kernel,
        out_shape=(jax.ShapeDtypeStruct((m * ROW_CHUNKS, LANES), jnp.uint32),
                   jax.ShapeDtypeStruct((m, LANES), jnp.int32),
                   jax.ShapeDtypeStruct((m, LANES), F32), jax.ShapeDtypeStruct((m, LANES), jnp.int32),
                   jax.ShapeDtypeStruct((1, LANES), F32)),
        grid=(m // tm,),
        in_specs=[
            pl.BlockSpec((tm, d), lambda i: (i, 0)),
            pl.BlockSpec((1, d), lambda i: (0, 0)),
            pl.BlockSpec((None, 1, d), mod_idx),
            pl.BlockSpec((None, 1, d), mod_idx),
            pl.BlockSpec((d, LANES), lambda i: (0, 0)),
            one_row,
            pl.BlockSpec((tm, tm), lambda i: (0, 0)),
            one_row,
        ],
        out_specs=(pl.BlockSpec((tm * ROW_CHUNKS, LANES), lambda i: (i, 0)), tok_lanes, tok_lanes, tok_lanes,
                   one_row),
        scratch_shapes=[pltpu.VMEM((1, LANES), F32)],
        compiler_params=_params("arbitrary"),
        name="route",
    )(x2d, norm_w.reshape(1, d), sc1p, sh, rw, rb, tri, cnt0)


def _half_k_dot(lo, hi, w_ref):
    half = lo.shape[1]
    return (jnp.dot(lo, w_ref[0:half, :], preferred_element_type=F32)
            + jnp.dot(hi, w_ref[half:, :], preferred_element_type=F32))


def _swiglu_packed(p, wg_ref, wu_ref, wd_ref):
    lo, hi = (v.astype(BF16) for v in _unpack_halves(p))
    hm = (_silu(_half_k_dot(lo, hi, wg_ref)) * _half_k_dot(lo, hi, wu_ref)).astype(BF16)
    return jnp.dot(hm, wd_ref[...], preferred_element_type=F32)


def _shared_ffn_kernel(h_ref, wg_ref, wu_ref, wd_ref, o_ref, acc_ref, wgb, wub, wdb):
    e = pl.program_id(1)
    wgb[...] = wg_ref[...].astype(BF16)
    wub[...] = wu_ref[...].astype(BF16)
    wdb[...] = wd_ref[...].astype(BF16)
    part = _swiglu_packed(_load_row_blocked(h_ref, acc_ref.shape[0]), wgb, wub, wdb)

    @pl.when(e == 0)
    def _():
        acc_ref[...] = part

    @pl.when(e > 0)
    def _():
        acc_ref[...] += part

    @pl.when(e == pl.num_programs(1) - 1)
    def _():
        o_ref[...] = acc_ref[...].astype(o_ref.dtype)


def shared_expert(h2p, wg, wu, wd, layer, tm=512):
    m = h2p.shape[0] // ROW_CHUNKS
    _, d, f = wg.shape
    tm = min(tm, m)
    fe = EXPERT_DIM
    return pl.pallas_call(
        _shared_ffn_kernel,
        out_shape=jax.ShapeDtypeStruct((m, d), BF16),
        grid=(m // tm, f // fe),
        in_specs=[
            pl.BlockSpec((tm * ROW_CHUNKS, LANES), lambda i, e: (i, 0)),
            pl.BlockSpec((None, d, fe), lambda i, e: (layer, 0, e)),
            pl.BlockSpec((None, d, fe), lambda i, e: (layer, 0, e)),
            pl.BlockSpec((None, fe, d), lambda i, e: (layer, e, 0)),
        ],
        out_specs=pl.BlockSpec((tm, d), lambda i, e: (i, 0)),
        scratch_shapes=[pltpu.VMEM((tm, d), F32), pltpu.VMEM((d, fe), BF16), pltpu.VMEM((d, fe), BF16),
                        pltpu.VMEM((fe, d), BF16)],
        compiler_params=_params("parallel", "arbitrary"),
        name="shared_expert",
    )(h2p, wg, wu, wd)


EXPERT_TILE = 256
COMBINE_TILE = 64


def dispatch_plan(ids, ranks, counts):
    n = ids.shape[0]
    a = n * TOP_K
    tm = EXPERT_TILE
    n_tiles = a // tm + N_EXPERTS
    tiles_per = (counts + tm - 1) // tm
    tile_end = jnp.cumsum(tiles_per)
    row_start = (tile_end - tiles_per) * tm
    experts = jnp.arange(N_EXPERTS, dtype=jnp.int32)
    row_of = ranks + jnp.sum(jnp.where(ids[..., None] == experts, row_start, 0), axis=-1)
    tok = jnp.arange(a, dtype=jnp.int32) // TOP_K
    row_tok = jnp.zeros((n_tiles * tm,), jnp.int32).at[row_of.reshape(a)].set(tok)
    n_used = tile_end[-1:]
    tile_e = jnp.minimum(jnp.searchsorted(tile_end, jnp.arange(n_tiles, dtype=jnp.int32), side="right"),
                         N_EXPERTS - 1).astype(jnp.int32)
    last_e = tile_e[jnp.maximum(n_used[0] - 1, 0)]
    tile_e = jnp.where(jnp.arange(n_tiles) < n_used[0], tile_e, last_e)
    return row_tok.reshape(n_tiles, 1, tm), tile_e, n_used.astype(jnp.int32), row_of.reshape(n, TOP_K)


def _row_block(r):
    return pl.ds(pl.multiple_of(r * ROW_CHUNKS, ROW_CHUNKS), ROW_CHUNKS)


def _expert_kernel(te_ref, nu_ref, cur_ref, nxt_ref, h_ref, wg_ref, wu_ref, wd_ref, o_ref,
                   xbuf, wgb, wub, wdb, sem):
    j = pl.program_id(0)
    n_used = nu_ref[0]
    slot = j % 2
    tm = xbuf.shape[1] // ROW_CHUNKS

    def gather(idx_ref, s):
        def body(r, carry):
            t = idx_ref[0, r]
            pltpu.make_async_copy(h_ref.at[_row_block(t), :], xbuf.at[s, _row_block(r), :], sem.at[s]).start()
            return carry
        lax.fori_loop(0, tm, body, 0, unroll=8)

    @pl.when(j == 0)
    def _():
        gather(cur_ref, 0)

    @pl.when(j + 1 < n_used)
    def _():
        gather(nxt_ref, 1 - slot)

    @pl.when(j < n_used)
    def _():
        @pl.when((j == 0) | (te_ref[j] != te_ref[jnp.maximum(j - 1, 0)]))
        def _():
            wgb[...] = wg_ref[...].astype(BF16)
            wub[...] = wu_ref[...].astype(BF16)
            wdb[...] = wd_ref[...].astype(BF16)

        pltpu.make_async_copy(xbuf.at[slot], xbuf.at[slot], sem.at[slot]).wait()
        x = _load_row_blocked(xbuf.at[slot], tm)
        _store_row_blocked(o_ref, _pack_halves(_swiglu_packed(x, wgb, wub, wdb)))

    @pl.when(j >= n_used)
    def _():
        o_ref[...] = jnp.zeros(o_ref.shape, o_ref.dtype)


def routed_experts(h2p, row_tok, tile_e, n_used, wg, wu, wd, layer):
    n_tiles, _, tm = row_tok.shape
    _, ne, d, f = wg.shape
    assert d == 2 * ROW_CHUNKS * LANES
    last = n_tiles - 1
    idx_spec = lambda shift: pl.BlockSpec((None, 1, tm), lambda j, te, nu: (jnp.minimum(j + shift, last), 0, 0),
                                          memory_space=pltpu.SMEM)
    return pl.pallas_call(
        _expert_kernel,
        out_shape=jax.ShapeDtypeStruct((n_tiles * tm * ROW_CHUNKS, LANES), jnp.uint32),
        grid_spec=pltpu.PrefetchScalarGridSpec(
            num_scalar_prefetch=2,
            grid=(n_tiles,),
            in_specs=[
                idx_spec(0), idx_spec(1),
                pl.BlockSpec(memory_space=pl.ANY),
                pl.BlockSpec((None, None, d, f), lambda j, te, nu: (layer, te[j], 0, 0)),
                pl.BlockSpec((None, None, d, f), lambda j, te, nu: (layer, te[j], 0, 0)),
                pl.BlockSpec((None, None, f, d), lambda j, te, nu: (layer, te[j], 0, 0)),
            ],
            out_specs=pl.BlockSpec((tm * ROW_CHUNKS, LANES), lambda j, te, nu: (j, 0)),
            scratch_shapes=[pltpu.VMEM((2, tm * ROW_CHUNKS, LANES), jnp.uint32), pltpu.VMEM((d, f), BF16),
                            pltpu.VMEM((d, f), BF16), pltpu.VMEM((f, d), BF16), pltpu.SemaphoreType.DMA((2,))],
        ),
        compiler_params=_params("arbitrary"),
        name="routed_experts",
    )(tile_e, n_used, row_tok, row_tok, h2p, wg, wu, wd)


def _combine_kernel(cur_ref, nxt_ref, ys_ref, w_ref, ysh_ref, x_ref, g_ref, nf_ref, o_ref, gbuf, sem, *,
                    final_norm):
    i = pl.program_id(0)
    slot = i % 2
    tmc = x_ref.shape[0]

    def gather(idx_ref, s):
        def body(r, carry):
            dst_rows = pl.ds(pl.multiple_of(r * ROW_CHUNKS, ROW_CHUNKS), ROW_CHUNKS)
            for k in range(TOP_K):
                t = idx_ref[0, r * TOP_K + k]
                src_rows = pl.ds(pl.multiple_of(t * ROW_CHUNKS, ROW_CHUNKS), ROW_CHUNKS)
                pltpu.make_async_copy(ys_ref.at[src_rows, :], gbuf.at[s, k, dst_rows, :], sem.at[s]).start()
            return carry
        lax.fori_loop(0, tmc, body, 0)

    @pl.when(i == 0)
    def _():
        gather(cur_ref, 0)

    @pl.when(i + 1 < pl.num_programs(0))
    def _():
        gather(nxt_ref, 1 - slot)

    pltpu.make_async_copy(gbuf.at[slot], gbuf.at[slot], sem.at[slot]).wait()
    w = w_ref[...]
    wk = [w[:, k:k + 1] for k in range(TOP_K)]
    cols_lo, cols_hi = [], []
    for c in range(ROW_CHUNKS):
        acc_lo = acc_hi = None
        for k in range(TOP_K):
            lo, hi = _unpack_halves(gbuf[slot, k, pl.ds(c, tmc, stride=ROW_CHUNKS), :])
            acc_lo = wk[k] * lo if acc_lo is None else acc_lo + wk[k] * lo
            acc_hi = wk[k] * hi if acc_hi is None else acc_hi + wk[k] * hi
        cols_lo.append(acc_lo)
        cols_hi.append(acc_hi)
    y = jnp.concatenate(cols_lo + cols_hi, axis=1) + ysh_ref[...].astype(F32)
    x = x_ref[...] + g_ref[...] * y
    if final_norm:
        x = x * lax.rsqrt(jnp.mean(x * x, axis=-1, keepdims=True) + EPS) * nf_ref[...]
    o_ref[...] = x


def combine_residual(x2d, ysh, ys, row_of, wts, gate, norm_final, rows_per_mod, row0, final_norm):
    m, d = x2d.shape
    tmc = COMBINE_TILE
    n_steps = m // tmc
    b0 = row0 // tmc
    idx = row_of.reshape(-1, 1, tmc * TOP_K)
    idx_spec = lambda shift: pl.BlockSpec(
        (None, 1, tmc * TOP_K), lambda i: (b0 + jnp.minimum(i + shift, n_steps - 1), 0, 0), memory_space=pltpu.SMEM)
    return pl.pallas_call(
        functools.partial(_combine_kernel, final_norm=final_norm),
        out_shape=jax.ShapeDtypeStruct((m, d), F32),
        grid=(n_steps,),
        in_specs=[
            idx_spec(0), idx_spec(1),
            pl.BlockSpec(memory_space=pl.ANY),
            pl.BlockSpec((tmc, LANES), lambda i: (b0 + i, 0)),
            pl.BlockSpec((tmc, d), lambda i: (b0 + i, 0)),
            pl.BlockSpec((tmc, d), lambda i: (i, 0)),
            pl.BlockSpec((None, 1, d), lambda i: ((i * tmc) // rows_per_mod, 0, 0)),
            pl.BlockSpec((1, d), lambda i: (0, 0)),
        ],
        out_specs=pl.BlockSpec((tmc, d), lambda i: (i, 0)),
        scratch_shapes=[pltpu.VMEM((2, TOP_K, tmc * ROW_CHUNKS, LANES), jnp.uint32),
                        pltpu.SemaphoreType.DMA((2,))],
        compiler_params=_params("arbitrary"),
        name="combine_residual",
    )(idx, idx, ys, wts, ysh, x2d, gate, norm_final.reshape(1, d))


def kernel(x, c, ctx, c_ctx, ada_w, ada_b, norm_mix, norm_ffn, w_in, dn_conv, dn_a_log, dn_dt_bias, dn_norm, attn_sink, pool_w, pool_scale, w_branch, w_out, router_w, router_bias, exp_gate, exp_up, exp_down, shared_gate, shared_up, shared_down, norm_final):
    batch, seq_len, d = x.shape
    n_ctx = ctx.shape[1]
    depth = ada_w.shape[0]
    cond_rows = 16
    cond = jnp.zeros((cond_rows, d), F32).at[:batch].set(c).at[batch].set(c_ctx)
    mod_all = ada_modulation(cond, ada_w, ada_b).reshape(depth, cond_rows, 6, 1, d)
    cos_t, sin_t = rope_tables(seq_len)
    xl = x.reshape(batch * seq_len, d)
    xc = ctx.reshape(batch * n_ctx, d)
    zero_state = jnp.zeros((batch, DN_HEADS, 2, HEAD_DIM, HEAD_DIM), F32)
    for i in range(depth):
        last = i == depth - 1
        sh1, sc1, g1, sh2, sc2, g2 = (mod_all[i, :, s] for s in range(6))
        lat = lambda m: m[:batch]
        cx = lambda m: m[batch:batch + 1]
        w_perm = permute_projection_columns(w_in[i].astype(BF16))
        wb = w_branch[i].astype(BF16)
        wo = w_out[i].astype(BF16)
        ctx_col0 = COL_KD if last else 0
        p_ctx = norm_project(xc, norm_mix[i], 1.0 + cx(sc1), cx(sh1), w_perm[:, ctx_col0:], batch * n_ctx)
        p_lat = norm_project(xl, norm_mix[i], 1.0 + lat(sc1), lat(sh1), w_perm, seq_len)

        y_dn_c, s_ctx = deltanet_mix(p_ctx, dn_a_log[i], dn_dt_bias[i], dn_conv[i], dn_norm[i], zero_state,
                                     batch, n_ctx, with_q=not last, col0=ctx_col0)
        if not last:
            branches_c = (fourier_mix(p_ctx, batch, n_ctx), y_dn_c,
                          pool_mix(p_ctx, pool_w[i], pool_scale[i], batch, n_ctx),
                          context_attention(p_ctx, attn_sink[i], batch, n_ctx))
            xc = out_project_residual(merge_gate(branches_c, p_ctx, wb), wo, xc, cx(g1), batch * n_ctx)

        y_dn, _ = deltanet_mix(p_lat, dn_a_log[i], dn_dt_bias[i], dn_conv[i], dn_norm[i], s_ctx, batch, seq_len)
        branches = (fourier_mix(p_lat, batch, seq_len), y_dn,
                    pool_mix(p_lat, pool_w[i], pool_scale[i], batch, seq_len),
                    latent_attention(p_lat, p_ctx, attn_sink[i], cos_t, sin_t, batch, seq_len, n_ctx, ctx_col0))
        xl = out_project_residual(merge_gate(branches, p_lat, wb), wo, xl, lat(g1), seq_len)

        no_tokens = jnp.zeros((1, LANES), F32)
        h2p, ids, wts, ranks, counts = route(xl, norm_ffn[i], 1.0 + lat(sc2), lat(sh2), router_w[i],
                                             router_bias[i], seq_len, no_tokens)
        if not last:
            *routed_c, counts = route(xc, norm_ffn[i], 1.0 + cx(sc2), cx(sh2), router_w[i], router_bias[i],
                                      batch * n_ctx, counts)
            h2p, ids, wts, ranks = (jnp.concatenate(pair, axis=0)
                                    for pair in zip((h2p, ids, wts, ranks), routed_c))
        row_tok, tile_e, n_used, row_of = dispatch_plan(ids[:, :TOP_K], ranks[:, :TOP_K],
                                                        counts[0, :N_EXPERTS].astype(jnp.int32))
        ys = routed_experts(h2p, row_tok, tile_e, n_used, exp_gate, exp_up, exp_down, i)
        ysh = shared_expert(h2p, shared_gate, shared_up, shared_down, i)
        n_lat = batch * seq_len
        if not last:
            xc = combine_residual(xc, ysh, ys, row_of, wts, cx(g2), norm_final, batch * n_ctx, n_lat, False)
        xl = combine_residual(xl, ysh, ys, row_of, wts, lat(g2), norm_final, seq_len, 0, last)
    return xl.reshape(batch, seq_len, d)
```

```python
import functools
import math

import jax
import jax.numpy as jnp
import numpy as np
from jax import lax
from jax.experimental import pallas as pl
from jax.experimental.pallas import tpu as pltpu

F32 = jnp.float32
BF16 = jnp.bfloat16

HEAD_DIM = 128
GRID_W = 64
ROPE_BASE = 10000.0
EPS = 1e-6
NEG_INF = -1e30
N_BRANCHES = 4
BRANCH_WIDTH = 1024
ATT_Q_HEADS = 8
ATT_KV_HEADS = 2
ATT_GROUP = ATT_Q_HEADS // ATT_KV_HEADS
WINDOW = 128
ATT_BLOCK = 128
DN_HEADS = 8
DN_WIDTH = DN_HEADS * HEAD_DIM
DN_CONV = 5
DN_CHUNK = 64
DN_SUPER = 256
FT_GROUPS = 4
FT_GROUP_DIM = BRANCH_WIDTH // FT_GROUPS
POOL_WINDOWS = (2, 4, 8, 16)
POOL_GROUP_DIM = BRANCH_WIDTH // len(POOL_WINDOWS)
N_EXPERTS = 64
N_EXPERT_GROUPS = 8
TOPK_GROUPS = 4
TOP_K = 8
EXPERT_DIM = 256
ROUTE_SCALE = 2.5

VMEM_LIMIT_BYTES = 56 * 1024 * 1024
LANES = 128

ATT_KV_W = ATT_KV_HEADS * HEAD_DIM
ATT_Q_W = ATT_Q_HEADS * HEAD_DIM
GATE_W = 16384
COL_KA = 0
COL_VA = COL_KA + ATT_KV_W
COL_KD = COL_VA + ATT_KV_W
COL_VD = COL_KD + DN_WIDTH
COL_AB = COL_VD + DN_WIDTH
COLS_CTX_SRC = COL_AB + 4 * DN_HEADS
COLS_CTX = 3072
CTX_PAD = COLS_CTX - COLS_CTX_SRC
COL_QA = COLS_CTX
COL_QD = COL_QA + ATT_Q_W
COL_ZD = COL_QD + DN_WIDTH
COL_FT = COL_ZD + DN_WIDTH
COL_POOL = COL_FT + BRANCH_WIDTH
COL_GATE = COL_POOL + BRANCH_WIDTH
COLS_PAD = COL_GATE + GATE_W


def _params(*sem):
    return pltpu.CompilerParams(dimension_semantics=sem, vmem_limit_bytes=VMEM_LIMIT_BYTES)


def _silu(x):
    return x * jax.nn.sigmoid(x)


def _ada_kernel(c_ref, w_ref, b_ref, o_ref):
    s = _silu(c_ref[...]).astype(BF16)
    o_ref[...] = jnp.dot(s, w_ref[...].astype(BF16), preferred_element_type=F32) + b_ref[...]


def ada_modulation(cond, ada_w, ada_b, tn=512):
    depth, d, n = ada_w.shape
    rows = cond.shape[0]
    return pl.pallas_call(
        _ada_kernel,
        out_shape=jax.ShapeDtypeStruct((depth, rows, n), F32),
        grid=(depth, n // tn),
        in_specs=[
            pl.BlockSpec((rows, d), lambda l, j: (0, 0)),
            pl.BlockSpec((None, d, tn), lambda l, j: (l, 0, j)),
            pl.BlockSpec((None, 1, tn), lambda l, j: (l, 0, j)),
        ],
        out_specs=pl.BlockSpec((None, rows, tn), lambda l, j: (l, 0, j)),
        compiler_params=_params("parallel", "parallel"),
        name="ada_modulation",
    )(cond, ada_w, ada_b.reshape(depth, 1, n))


def _modulated_norm(x, g, sc1p, sh):
    y = x * lax.rsqrt(jnp.mean(x * x, axis=-1, keepdims=True) + EPS) * g
    return y * sc1p + sh


def _norm_kernel(x_ref, g_ref, sc_ref, sh_ref, o_ref):
    o_ref[...] = _modulated_norm(x_ref[...], g_ref[...], sc_ref[...], sh_ref[...]).astype(o_ref.dtype)


def _matmul_kernel(a_ref, w_ref, o_ref):
    o_ref[...] = jnp.dot(a_ref[...], w_ref[...], preferred_element_type=F32).astype(o_ref.dtype)


def norm_project(x2d, norm_w, sc1p, sh, w, rows_per_mod, tm=1024, tn=1024, out_dtype=F32):
    m, d = x2d.shape
    n = w.shape[1]
    tr = min(256, m)
    mod_idx = lambda i: ((i * tr) // rows_per_mod, 0, 0)
    hn = pl.pallas_call(
        _norm_kernel,
        out_shape=jax.ShapeDtypeStruct((m, d), BF16),
        grid=(m // tr,),
        in_specs=[
            pl.BlockSpec((tr, d), lambda i: (i, 0)),
            pl.BlockSpec((1, d), lambda i: (0, 0)),
            pl.BlockSpec((None, 1, d), mod_idx),
            pl.BlockSpec((None, 1, d), mod_idx),
        ],
        out_specs=pl.BlockSpec((tr, d), lambda i: (i, 0)),
        compiler_params=_params("parallel"),
        name="modulated_norm",
    )(x2d, norm_w.reshape(1, d), sc1p, sh)
    tm = min(tm, m)
    return pl.pallas_call(
        _matmul_kernel,
        out_shape=jax.ShapeDtypeStruct((m, n), out_dtype),
        grid=(m // tm, n // tn),
        in_specs=[
            pl.BlockSpec((tm, d), lambda i, j: (i, 0)),
            pl.BlockSpec((d, tn), lambda i, j: (0, j)),
        ],
        out_specs=pl.BlockSpec((tm, tn), lambda i, j: (i, j)),
        compiler_params=_params("parallel", "arbitrary"),
        name="project",
    )(hn, w)


def _rope(x, cos, sin_signed):
    lane = lax.broadcasted_iota(jnp.int32, x.shape, 1)
    swapped = jnp.where((lane % 64) < 32, pltpu.roll(x, 96, 1), pltpu.roll(x, 32, 1))
    return x * cos + swapped * sin_signed


def _attn_kernel(*refs, banded, n_ctx, seq_len):
    if banded:
        (sink_ref, q_ref, kp_ref, kc_ref, kn_ref, vp_ref, vc_ref, vn_ref, kx_ref, vx_ref,
         cos_ref, sin_ref, o_ref) = refs
    else:
        sink_ref, q_ref, kx_ref, vx_ref, o_ref = refs
    blk = q_ref.shape[0]
    scale = HEAD_DIM ** -0.5
    n = pl.program_id(1)
    if banded:
        def table(ref, blk_idx):
            start = pl.multiple_of(blk_idx * blk, blk)
            return ref[pl.ds(start, blk), :]
        nblocks = seq_len // blk
        ip = jnp.maximum(n - 1, 0)
        inx = jnp.minimum(n + 1, nblocks - 1)
        cos_q, sin_q = table(cos_ref, n), table(sin_ref, n)
        cos_p, sin_p = table(cos_ref, ip), table(sin_ref, ip)
        cos_n, sin_n = table(cos_ref, inx), table(sin_ref, inx)
        rows = lax.broadcasted_iota(jnp.int32, (ATT_GROUP * blk, 3 * blk), 0) % blk
        cols = lax.broadcasted_iota(jnp.int32, (ATT_GROUP * blk, 3 * blk), 1)
        rel = cols - blk - rows
        kpos = (n - 1) * blk + cols
        valid = (jnp.abs(rel) <= WINDOW) & (kpos >= 0) & (kpos < seq_len)
    for h in range(ATT_KV_HEADS):
        hs = slice(h * HEAD_DIM, (h + 1) * HEAD_DIM)
        qs, sinks = [], []
        for g in range(ATT_GROUP):
            hq = h * ATT_GROUP + g
            qh = q_ref[:, hq * HEAD_DIM:(hq + 1) * HEAD_DIM]
            if banded:
                qh = _rope(qh, cos_q, sin_q)
            qs.append((qh * scale).astype(BF16))
            sinks.append(jnp.full((blk, 1), sink_ref[hq], F32))
        q4 = jnp.concatenate(qs, axis=0)
        sink = jnp.concatenate(sinks, axis=0)
        kx = kx_ref[:, hs].astype(BF16)
        vx = vx_ref[:, hs].astype(BF16)
        s_ctx = lax.dot_general(q4, kx, (((1,), (1,)), ((), ())), preferred_element_type=F32)
        m = jnp.maximum(jnp.max(s_ctx, axis=-1, keepdims=True), sink)
        if banded:
            kband = jnp.concatenate([
                _rope(kp_ref[:, hs], cos_p, sin_p), _rope(kc_ref[:, hs], cos_q, sin_q),
                _rope(kn_ref[:, hs], cos_n, sin_n)], axis=0).astype(BF16)
            vband = jnp.concatenate([vp_ref[:, hs], vc_ref[:, hs], vn_ref[:, hs]], axis=0).astype(BF16)
            s_loc = lax.dot_general(q4, kband, (((1,), (1,)), ((), ())), preferred_element_type=F32)
            s_loc = jnp.where(valid, s_loc, NEG_INF)
            m = jnp.maximum(m, jnp.max(s_loc, axis=-1, keepdims=True))
        p_ctx = jnp.exp(s_ctx - m)
        denom = jnp.sum(p_ctx, axis=-1, keepdims=True) + jnp.exp(sink - m)
        o = jnp.dot(p_ctx.astype(BF16), vx, preferred_element_type=F32)
        if banded:
            p_loc = jnp.exp(s_loc - m)
            denom = denom + jnp.sum(p_loc, axis=-1, keepdims=True)
            o = o + jnp.dot(p_loc.astype(BF16), vband, preferred_element_type=F32)
        o = o / denom
        for g in range(ATT_GROUP):
            hq = h * ATT_GROUP + g
            o_ref[:, hq * HEAD_DIM:(hq + 1) * HEAD_DIM] = o[g * blk:(g + 1) * blk].astype(o_ref.dtype)


def latent_attention(p_lat, p_ctx, sink, cos_t, sin_t, batch, seq_len, n_ctx):
    blk = ATT_BLOCK
    nb = seq_len // blk
    cq = COL_QA // ATT_Q_W
    ck, cv = COL_KA // ATT_KV_W, COL_VA // ATT_KV_W
    xk, xv = ck, cv
    row = lambda b, n: b * nb + n
    kspec = lambda col, shift: pl.BlockSpec(
        (blk, ATT_KV_W), lambda b, n: (row(b, jnp.clip(n + shift, 0, nb - 1)), col))
    return pl.pallas_call(
        functools.partial(_attn_kernel, banded=True, n_ctx=n_ctx, seq_len=seq_len),
        out_shape=jax.ShapeDtypeStruct((batch * seq_len, ATT_Q_W), BF16),
        grid=(batch, nb),
        in_specs=[
            pl.BlockSpec(memory_space=pltpu.SMEM),
            pl.BlockSpec((blk, ATT_Q_W), lambda b, n: (row(b, n), cq)),
            kspec(ck, -1), kspec(ck, 0), kspec(ck, 1),
            kspec(cv, -1), kspec(cv, 0), kspec(cv, 1),
            pl.BlockSpec((n_ctx, ATT_KV_W), lambda b, n: (b, xk)),
            pl.BlockSpec((n_ctx, ATT_KV_W), lambda b, n: (b, xv)),
            pl.BlockSpec((seq_len, HEAD_DIM), lambda b, n: (0, 0)),
            pl.BlockSpec((seq_len, HEAD_DIM), lambda b, n: (0, 0)),
        ],
        out_specs=pl.BlockSpec((blk, ATT_Q_W), lambda b, n: (row(b, n), 0)),
        compiler_params=_params("parallel", "arbitrary"),
        name="latent_attention",
    )(sink, p_lat, p_lat, p_lat, p_lat, p_lat, p_lat, p_lat, p_ctx, p_ctx, cos_t, sin_t)


def context_attention(p_ctx, sink, batch, n_ctx):
    cq = COL_QA // ATT_Q_W
    ck, cv = COL_KA // ATT_KV_W, COL_VA // ATT_KV_W
    return pl.pallas_call(
        functools.partial(_attn_kernel, banded=False, n_ctx=n_ctx, seq_len=n_ctx),
        out_shape=jax.ShapeDtypeStruct((batch * n_ctx, ATT_Q_W), BF16),
        grid=(batch, 1),
        in_specs=[
            pl.BlockSpec(memory_space=pltpu.SMEM),
            pl.BlockSpec((n_ctx, ATT_Q_W), lambda b, n: (b, cq)),
            pl.BlockSpec((n_ctx, ATT_KV_W), lambda b, n: (b, ck)),
            pl.BlockSpec((n_ctx, ATT_KV_W), lambda b, n: (b, cv)),
        ],
        out_specs=pl.BlockSpec((n_ctx, ATT_Q_W), lambda b, n: (b, 0)),
        compiler_params=_params("parallel", "arbitrary"),
        name="context_attention",
    )(sink, p_ctx, p_ctx, p_ctx)


def rope_tables(seq_len):
    rows = seq_len // GRID_W
    row = jnp.repeat(jnp.arange(rows, dtype=F32), GRID_W)
    col = jnp.tile(jnp.arange(GRID_W, dtype=F32), rows)
    half = HEAD_DIM // 2
    inv_freq = ROPE_BASE ** (-jnp.arange(0, half, 2, dtype=F32) / half)
    ang_r, ang_c = row[:, None] * inv_freq, col[:, None] * inv_freq
    cos_t = jnp.concatenate([jnp.cos(ang_r)] * 2 + [jnp.cos(ang_c)] * 2, axis=-1)
    sin_t = jnp.concatenate([-jnp.sin(ang_r), jnp.sin(ang_r), -jnp.sin(ang_c), jnp.sin(ang_c)], axis=-1)
    return cos_t, sin_t


def _fourier_kernel(u_ref, cc_ref, sc_ref, ct_ref, st_ref, o_ref, p_ref, q_ref, *, scale):
    @pl.when(pl.program_id(2) == 0)
    def _():
        u = u_ref[...].astype(BF16)
        p_ref[...] = jnp.dot(u, cc_ref[...], preferred_element_type=F32).astype(BF16)
        q_ref[...] = jnp.dot(u, sc_ref[...], preferred_element_type=F32).astype(BF16)

    re = (jnp.dot(ct_ref[...], p_ref[...], preferred_element_type=F32)
          - jnp.dot(st_ref[...], q_ref[...], preferred_element_type=F32))
    o_ref[...] = (re * scale).astype(o_ref.dtype)


def dft_tables(n):
    k = jnp.arange(n, dtype=jnp.int32)
    ang = ((k[:, None] * k[None, :]) % n).astype(F32) * (2.0 * math.pi / n)
    return jnp.cos(ang).astype(BF16), jnp.sin(ang).astype(BF16)


def fourier_mix(p2d, batch, seq_len, tr=512):
    tr = min(tr, seq_len)
    cc, sc = dft_tables(FT_GROUP_DIM)
    ct, st = dft_tables(seq_len)
    c0 = COL_FT // FT_GROUP_DIM
    nr = seq_len // tr
    return pl.pallas_call(
        functools.partial(_fourier_kernel, scale=1.0 / math.sqrt(seq_len * FT_GROUP_DIM)),
        out_shape=jax.ShapeDtypeStruct((batch * seq_len, BRANCH_WIDTH), BF16),
        grid=(batch, FT_GROUPS, nr),
        in_specs=[
            pl.BlockSpec((seq_len, FT_GROUP_DIM), lambda b, g, r: (b, c0 + g)),
            pl.BlockSpec((FT_GROUP_DIM, FT_GROUP_DIM), lambda b, g, r: (0, 0)),
            pl.BlockSpec((FT_GROUP_DIM, FT_GROUP_DIM), lambda b, g, r: (0, 0)),
            pl.BlockSpec((tr, seq_len), lambda b, g, r: (r, 0)),
            pl.BlockSpec((tr, seq_len), lambda b, g, r: (r, 0)),
        ],
        out_specs=pl.BlockSpec((tr, FT_GROUP_DIM), lambda b, g, r: (b * nr + r, g)),
        scratch_shapes=[pltpu.VMEM((seq_len, FT_GROUP_DIM), BF16)] * 2,
        compiler_params=_params("parallel", "parallel", "arbitrary"),
        name="fourier_mix",
    )(p2d, cc, sc, ct, st)


POOL_TILE = 256
POOL_HALO = 128


def _pool_kernel(u_ref, band_ref, w_ref, s_ref, o_ref, pad_ref, *, seq_len):
    g = pl.program_id(1)
    zeros = jnp.zeros((POOL_HALO, POOL_GROUP_DIM), BF16)
    pad_ref[0:POOL_HALO, :] = zeros
    pad_ref[POOL_HALO + seq_len:POOL_HALO + seq_len + POOL_HALO, :] = zeros
    pad_ref[POOL_HALO:POOL_HALO + seq_len, :] = u_ref[...].astype(BF16)
    half = jnp.left_shift(1, g)
    for t in range(seq_len // POOL_TILE):
        r0 = t * POOL_TILE
        win = pad_ref[r0:r0 + POOL_TILE + 2 * POOL_HALO, :]
        sums = jnp.dot(band_ref[...], win, preferred_element_type=F32)
        pos = r0 + lax.broadcasted_iota(jnp.int32, (POOL_TILE, 1), 0)
        cnt = jnp.minimum(pos + half, seq_len) - jnp.maximum(pos - half, 0)
        m = sums / cnt.astype(F32) - u_ref[r0:r0 + POOL_TILE, :]
        y = jnp.dot(m.astype(BF16), w_ref[...], preferred_element_type=F32) * s_ref[...]
        o_ref[r0:r0 + POOL_TILE, :] = y.astype(o_ref.dtype)


def pool_bands():
    r = np.arange(POOL_TILE)[:, None]
    c = np.arange(POOL_TILE + 2 * POOL_HALO)[None, :]
    rel = c - POOL_HALO - r
    bands = [((rel >= -(w // 2)) & (rel <= w // 2 - 1)) for w in POOL_WINDOWS]
    return jnp.asarray(np.stack(bands).astype(np.float32), dtype=BF16)


def pool_mix(p2d, pool_w, pool_scale, batch, seq_len):
    c0 = COL_POOL // POOL_GROUP_DIM
    ng = len(POOL_WINDOWS)
    return pl.pallas_call(
        functools.partial(_pool_kernel, seq_len=seq_len),
        out_shape=jax.ShapeDtypeStruct((batch * seq_len, BRANCH_WIDTH), BF16),
        grid=(batch, ng),
        in_specs=[
            pl.BlockSpec((seq_len, POOL_GROUP_DIM), lambda b, g: (b, c0 + g)),
            pl.BlockSpec((None, POOL_TILE, POOL_TILE + 2 * POOL_HALO), lambda b, g: (g, 0, 0)),
            pl.BlockSpec((None, POOL_GROUP_DIM, POOL_GROUP_DIM), lambda b, g: (g, 0, 0)),
            pl.BlockSpec((1, POOL_GROUP_DIM), lambda b, g: (0, g)),
        ],
        out_specs=pl.BlockSpec((seq_len, POOL_GROUP_DIM), lambda b, g: (b, g)),
        scratch_shapes=[pltpu.VMEM((seq_len + 2 * POOL_HALO, POOL_GROUP_DIM), BF16)],
        compiler_params=_params("parallel", "parallel"),
        name="pool_mix",
    )(p2d, pool_bands(), pool_w.astype(BF16), pool_scale.reshape(1, BRANCH_WIDTH))


def permute_projection_columns(w):
    pad = jnp.zeros(w.shape[:-1] + (CTX_PAD,), w.dtype)
    return jnp.concatenate([w[..., :COLS_CTX_SRC], pad, w[..., COLS_CTX_SRC:]], axis=-1)


CONV_PAD = 8
N_LEVELS = 6


def dn_masks():
    i = np.arange(DN_SUPER)[:, None]
    j = np.arange(DN_SUPER)[None, :]
    incl, lev = [], []
    for later in (lambda a, b: a >= b, lambda a, b: a <= b):
        strict = later(i, j) & (i != j)
        incl.append((i // DN_CHUNK == j // DN_CHUNK) & later(i, j))
        lv = [(i // 2 == j // 2) & strict]
        sz = 2
        while sz < DN_CHUNK:
            lv.append((i // (2 * sz) == j // (2 * sz)) & (i // sz != j // sz) & strict)
            sz *= 2
        lev.append(np.stack(lv))
    return (jnp.asarray(np.stack(incl).astype(np.float32)), jnp.asarray(np.stack(lev).astype(np.float32)))


def _split3(x):
    x1 = x.astype(BF16)
    r1 = x - x1.astype(F32)
    x2 = r1.astype(BF16)
    x3 = (r1 - x2.astype(F32)).astype(BF16)
    return x1, x2, x3


def _softplus(x):
    return jnp.maximum(x, 0.0) + jnp.log1p(jnp.exp(-jnp.abs(x)))


def _dn_kernel(alog_ref, dtb_ref, q_ref, k_ref, v_ref, z_ref, ab_ref, cq_ref, ck_ref, cv_ref, nw_ref,
               s0_ref, incl_ref, lev_ref, o_ref, sfin_ref,
               pq_ref, pk_ref, pv_ref, qe_ref, g_ref, b_ref, eg_ref, oacc_ref, *, seq_len, with_q):
    h = pl.program_id(1)
    n_super = seq_len // DN_SUPER
    n_chunks = seq_len // DN_CHUNK
    per_super = DN_SUPER // DN_CHUNK
    zpad = jnp.zeros((CONV_PAD, HEAD_DIM), F32)
    streams = [(pk_ref, k_ref), (pv_ref, v_ref)] + ([(pq_ref, q_ref)] if with_q else [])
    for pad_ref, src_ref in streams:
        pad_ref[0:CONV_PAD, :] = zpad
        pad_ref[CONV_PAD + seq_len:2 * CONV_PAD + seq_len, :] = zpad
        pad_ref[CONV_PAD:CONV_PAD + seq_len, :] = src_ref[...]

    lane = lax.broadcasted_iota(jnp.int32, (1, HEAD_DIM), 1)
    pick = lambda ref: jnp.where(lane == 0, ref[0, h], jnp.where(lane == 1, ref[1, h], 0.0))
    neg_a = -jnp.exp(pick(alog_ref))
    dtb = pick(dtb_ref)
    eye = (lax.broadcasted_iota(jnp.int32, (DN_SUPER, DN_SUPER), 0)
           == lax.broadcasted_iota(jnp.int32, (DN_SUPER, DN_SUPER), 1)).astype(F32)
    half = DN_CONV // 2

    def conv_silu(pad_ref, cw_ref, r0):
        win = pad_ref[pl.ds(r0, DN_SUPER + 2 * CONV_PAD), :]
        acc = None
        for j in range(DN_CONV):
            off = CONV_PAD - half + j
            term = cw_ref[j:j + 1, :] * win[off:off + DN_SUPER, :]
            acc = term if acc is None else acc + term
        return _silu(acc)

    def l2n(x):
        return x * lax.rsqrt(jnp.sum(x * x, axis=-1, keepdims=True) + EPS)

    def chain_setup(sidx):
        r0 = pl.multiple_of(sidx * DN_SUPER, DN_SUPER)
        k = l2n(conv_silu(pk_ref, ck_ref, r0))
        v = conv_silu(pv_ref, cv_ref, r0)
        kb = k.astype(BF16)
        kk = lax.dot_general(kb, kb, (((1,), (1,)), ((), ())), preferred_element_type=F32)
        q = qk = None
        if with_q:
            q = l2n(conv_silu(pq_ref, cq_ref, r0)) * (HEAD_DIM ** -0.5)
            qk = lax.dot_general(q.astype(BF16), kb, (((1,), (1,)), ((), ())), preferred_element_type=F32)
        ab = ab_ref[pl.ds(r0, DN_SUPER), :]
        gt = neg_a * _softplus(ab + dtb)
        bt = jax.nn.sigmoid(ab)
        gparts = _split3(gt)
        gcs = []
        for d in range(2):
            seg = incl_ref[d].astype(BF16)
            gcs.append(sum(jnp.dot(seg, gp, preferred_element_type=F32) for gp in gparts))
        gtot_t = gcs[0] + gcs[1] - gt
        chains = []
        for d in range(2):
            gcol = gcs[d][:, d:d + 1]
            grow = jnp.transpose(gcs[d])[d:d + 1, :]
            beta = bt[:, 2 + d:3 + d]
            e = jnp.exp((gcol - grow) * incl_ref[d])
            nf = (beta * kk) * e
            chains.append(dict(d=d, sidx=sidx, r0=r0, k=k, v=v, q=q, qk=qk, gcol=gcol, gtot=gtot_t[:, d:d + 1],
                               beta=beta, e=e, nf=nf, t=eye - nf * lev_ref[d, 0]))
        return chains

    def chain_finish(ch):
        d, k, gcol, gtot, beta = ch["d"], ch["k"], ch["gcol"], ch["gtot"], ch["beta"]
        egc = jnp.exp(gcol)
        rhs = jnp.concatenate([beta * ch["v"], (beta * egc) * k], axis=1).astype(BF16)
        sol = jnp.dot(ch["t"].astype(BF16), rhs, preferred_element_type=F32)
        u, w = sol[:, :HEAD_DIM], sol[:, HEAD_DIM:]
        wu = jnp.concatenate([w, u], axis=1).astype(BF16)
        kd = (k * jnp.exp(gtot - gcol)).astype(BF16)
        if with_q:
            qkm = (ch["qk"] * ch["e"] * incl_ref[d]).astype(BF16)
            r2 = jnp.dot(qkm, wu, preferred_element_type=F32)
            rows = pl.ds(ch["r0"], DN_SUPER)
            qe_ref[d, rows, :] = (ch["q"] * egc - r2[:, :HEAD_DIM]).astype(BF16)
            oacc_ref[d, rows, :] = r2[:, HEAD_DIM:]
        eg = jnp.exp(gtot)
        for c in range(per_super):
            cs = slice(c * DN_CHUNK, (c + 1) * DN_CHUNK)
            gb = lax.dot_general(kd[cs], wu[cs], (((0,), (0,)), ((), ())), preferred_element_type=F32)
            chunk = ch["sidx"] * per_super + c
            srows = pl.ds(pl.multiple_of(chunk * HEAD_DIM, HEAD_DIM), HEAD_DIM)
            g_ref[d, srows, :] = gb[:, :HEAD_DIM].astype(BF16)
            b_ref[d, srows, :] = gb[:, HEAD_DIM:]
            blk = jnp.broadcast_to(eg[c * DN_CHUNK:c * DN_CHUNK + 8, :], (8, HEAD_DIM))
            eg_ref[d, pl.ds(pl.multiple_of(chunk * 8, 8), 8), :] = blk

    group = 2 if n_super % 2 == 0 else 1

    def super_group(gi, carry):
        chains = [ch for s in range(group) for ch in chain_setup(gi * group + s)]
        for m in range(1, N_LEVELS):
            for ch in chains:
                ch["tb"] = ch["t"].astype(BF16)
                ch["x"] = jnp.dot((ch["nf"] * lev_ref[ch["d"], m]).astype(BF16), ch["tb"],
                                  preferred_element_type=F32)
            for ch in chains:
                ch["t"] = ch["t"] - jnp.dot(ch["tb"], ch["x"].astype(BF16), preferred_element_type=F32)
        for ch in chains:
            chain_finish(ch)
        return carry

    lax.fori_loop(0, n_super // group, super_group, 0)

    def chunk_step(c, states):
        ccs = (c, n_chunks - 1 - c)
        prods = []
        for d in range(2):
            srows = pl.ds(pl.multiple_of(ccs[d] * HEAD_DIM, HEAD_DIM), HEAD_DIM)
            lhs = g_ref[d, srows, :]
            if with_q:
                rows = pl.ds(pl.multiple_of(ccs[d] * DN_CHUNK, DN_CHUNK), DN_CHUNK)
                lhs = jnp.concatenate([lhs, qe_ref[d, rows, :]], axis=0)
            prods.append(jnp.dot(lhs, states[d].astype(BF16), preferred_element_type=F32))
        new_states = []
        for d in range(2):
            srows = pl.ds(pl.multiple_of(ccs[d] * HEAD_DIM, HEAD_DIM), HEAD_DIM)
            decay = eg_ref[d, pl.ds(pl.multiple_of(ccs[d] * 8, 8), 8), :][0:1, :]
            new_states.append(decay * states[d] - prods[d][:HEAD_DIM] + b_ref[d, srows, :])
            if with_q:
                rows = pl.ds(pl.multiple_of(ccs[d] * DN_CHUNK, DN_CHUNK), DN_CHUNK)
                oacc_ref[d, rows, :] += prods[d][HEAD_DIM:]
        return tuple(new_states)

    s_f, s_b = lax.fori_loop(0, n_chunks, chunk_step, (s0_ref[0], s0_ref[1]))
    sfin_ref[0] = s_f
    sfin_ref[1] = s_b
    if with_q:
        o = oacc_ref[0] + oacc_ref[1]
        on = o * lax.rsqrt(jnp.mean(o * o, axis=-1, keepdims=True) + EPS) * nw_ref[...]
        o_ref[...] = (on * _silu(z_ref[...])).astype(o_ref.dtype)
    else:
        o_ref[...] = jnp.zeros(o_ref.shape, o_ref.dtype)


def head_gate_layout(p2d):
    m = p2d.shape[0]
    c = COL_AB
    ab = p2d[:, c:c + 4 * DN_HEADS].astype(F32).reshape(m, 4, DN_HEADS)
    ab = jnp.transpose(ab, (0, 2, 1))
    ab = jnp.pad(ab, ((0, 0), (0, 0), (0, LANES - 4)))
    return ab.reshape(m, DN_HEADS * LANES)


def deltanet_mix(p2d, a_log, dt_bias, conv_w, norm_w, s0, batch, seq_len, with_q=True):
    incl, lev = dn_masks()
    ab = head_gate_layout(p2d)
    cols = (COL_QD, COL_KD, COL_VD, COL_ZD) if with_q else (COL_KD, COL_KD, COL_VD, COL_KD)
    cqb, ckb, cvb, czb = (c // HEAD_DIM for c in cols)
    seq_spec = lambda c0: pl.BlockSpec((seq_len, HEAD_DIM), lambda b, h: (b, c0 + h))
    conv_spec = lambda c0: pl.BlockSpec((DN_CONV, HEAD_DIM), lambda b, h: (0, c0 + h))
    state_spec = pl.BlockSpec((None, None, 2, HEAD_DIM, HEAD_DIM), lambda b, h: (b, h, 0, 0, 0))
    smem = pl.BlockSpec(memory_space=pltpu.SMEM)
    n_chunks = seq_len // DN_CHUNK
    per_dir = lambda dt: pltpu.VMEM((2, seq_len, HEAD_DIM), dt)
    pad = pltpu.VMEM((seq_len + 2 * CONV_PAD, HEAD_DIM), F32)
    return pl.pallas_call(
        functools.partial(_dn_kernel, seq_len=seq_len, with_q=with_q),
        out_shape=(jax.ShapeDtypeStruct((batch * seq_len, DN_WIDTH), BF16),
                   jax.ShapeDtypeStruct((batch, DN_HEADS, 2, HEAD_DIM, HEAD_DIM), F32)),
        grid=(batch, DN_HEADS),
        in_specs=[
            smem, smem,
            seq_spec(cqb), seq_spec(ckb), seq_spec(cvb), seq_spec(czb),
            pl.BlockSpec((seq_len, LANES), lambda b, h: (b, h)),
            conv_spec(0), conv_spec(DN_HEADS), conv_spec(2 * DN_HEADS),
            pl.BlockSpec((1, HEAD_DIM), lambda b, h: (0, 0)),
            state_spec,
            pl.BlockSpec((2, DN_SUPER, DN_SUPER), lambda b, h: (0, 0, 0)),
            pl.BlockSpec((2, N_LEVELS, DN_SUPER, DN_SUPER), lambda b, h: (0, 0, 0, 0)),
        ],
        out_specs=(pl.BlockSpec((seq_len, HEAD_DIM), lambda b, h: (b, h)), state_spec),
        scratch_shapes=[pad, pad, pad, per_dir(BF16),
                        pltpu.VMEM((2, n_chunks * HEAD_DIM, HEAD_DIM), BF16),
                        pltpu.VMEM((2, n_chunks * HEAD_DIM, HEAD_DIM), F32),
                        pltpu.VMEM((2, n_chunks * 8, HEAD_DIM), F32), per_dir(F32)],
        compiler_params=_params("parallel", "parallel"),
        name="deltanet_mix",
    )(a_log, dt_bias, p2d, p2d, p2d, p2d, ab, conv_w, conv_w, conv_w, norm_w.reshape(1, HEAD_DIM),
      s0, incl, lev)


def _merge_gate_kernel(y0, y1, y2, y3, g0, g1, g2, g3, wb_ref, o_ref):
    acc = None
    for n, (y_ref, g_ref) in enumerate(((y0, g0), (y1, g1), (y2, g2), (y3, g3))):
        t = jnp.dot(y_ref[...], wb_ref[n], preferred_element_type=F32) * jax.nn.sigmoid(g_ref[...].astype(F32))
        acc = t if acc is None else acc + t
    o_ref[...] = acc.astype(o_ref.dtype)


def merge_gate(branches, p2d, wb, tm=1024, tn=512):
    m = p2d.shape[0]
    d = wb.shape[-1]
    tm = min(tm, m)
    nj = d // tn
    y_spec = pl.BlockSpec((tm, BRANCH_WIDTH), lambda i, j: (i, 0))
    g_spec = lambda n: pl.BlockSpec((tm, tn), lambda i, j: (i, COL_GATE // tn + n * nj + j))
    return pl.pallas_call(
        _merge_gate_kernel,
        out_shape=jax.ShapeDtypeStruct((m, d), BF16),
        grid=(m // tm, nj),
        in_specs=[y_spec] * N_BRANCHES + [g_spec(n) for n in range(N_BRANCHES)]
        + [pl.BlockSpec((N_BRANCHES, BRANCH_WIDTH, tn), lambda i, j: (0, 0, j))],
        out_specs=pl.BlockSpec((tm, tn), lambda i, j: (i, j)),
        compiler_params=_params("parallel", "arbitrary"),
        name="merge_gate",
    )(*branches, p2d, p2d, p2d, p2d, wb)


def _out_proj_kernel(a_ref, w_ref, x_ref, g_ref, o_ref):
    o_ref[...] = x_ref[...] + g_ref[...] * jnp.dot(a_ref[...], w_ref[...], preferred_element_type=F32)


def out_project_residual(a, w, x2d, gate, rows_per_mod, tm=1024, tn=512):
    m, d = x2d.shape
    tm = min(tm, m)
    return pl.pallas_call(
        _out_proj_kernel,
        out_shape=jax.ShapeDtypeStruct((m, d), F32),
        grid=(m // tm, d // tn),
        in_specs=[
            pl.BlockSpec((tm, a.shape[1]), lambda i, j: (i, 0)),
            pl.BlockSpec((a.shape[1], tn), lambda i, j: (0, j)),
            pl.BlockSpec((tm, tn), lambda i, j: (i, j)),
            pl.BlockSpec((None, 1, tn), lambda i, j: ((i * tm) // rows_per_mod, 0, j)),
        ],
        out_specs=pl.BlockSpec((tm, tn), lambda i, j: (i, j)),
        compiler_params=_params("parallel", "arbitrary"),
        name="out_project_residual",
    )(a, w, x2d, gate)


def _first_argmax(v, lane):
    m = jnp.max(v, axis=-1, keepdims=True)
    idx = jnp.min(jnp.where(v == m, lane, LANES), axis=-1, keepdims=True)
    return m, idx


HI_MASK = 0xFFFF0000


def _pack_halves(v):
    half = v.shape[1] // 2
    lo = pltpu.bitcast(v[:, :half].astype(BF16).astype(F32), jnp.uint32)
    hi = pltpu.bitcast(v[:, half:].astype(BF16).astype(F32), jnp.uint32)
    return hi | (lo >> 16)


def _unpack_halves(p):
    return pltpu.bitcast(p << 16, F32), pltpu.bitcast(p & jnp.uint32(HI_MASK), F32)


ROW_CHUNKS = 16


def _store_row_blocked(ref, packed):
    rows = packed.shape[0]
    for c in range(ROW_CHUNKS):
        ref[pl.ds(c, rows, stride=ROW_CHUNKS), :] = packed[:, c * LANES:(c + 1) * LANES]


def _load_row_blocked(ref, rows):
    return jnp.concatenate([ref[pl.ds(c, rows, stride=ROW_CHUNKS), :] for c in range(ROW_CHUNKS)], axis=1)


def _route_kernel(x_ref, g_ref, sc_ref, sh_ref, rw_ref, rb_ref, tri_ref, cnt0_ref,
                  h_ref, ids_ref, wts_ref, rank_ref, cnt_ref, run_ref):
    h = _modulated_norm(x_ref[...], g_ref[...], sc_ref[...], sh_ref[...])
    _store_row_blocked(h_ref, _pack_halves(h))
    logits = jnp.dot(h, rw_ref[...], preferred_element_type=F32, precision=lax.Precision.HIGHEST)
    scores = jax.nn.sigmoid(logits)
    lane = lax.broadcasted_iota(jnp.int32, scores.shape, 1)
    grp = lane // (N_EXPERTS // N_EXPERT_GROUPS)
    ninf = -jnp.inf
    vb = jnp.where(lane < N_EXPERTS, scores + rb_ref[...], ninf)
    gs = jnp.full(scores.shape, ninf, F32)
    for g in range(N_EXPERT_GROUPS):
        vg = jnp.where(grp == g, vb, ninf)
        m1, i1 = _first_argmax(vg, lane)
        m2 = jnp.max(jnp.where(lane == i1, ninf, vg), axis=-1, keepdims=True)
        gs = jnp.where(lane == g, m1 + m2, gs)
    allowed = jnp.zeros(scores.shape, F32)
    for _ in range(TOPK_GROUPS):
        _, gi = _first_argmax(gs, lane)
        gs = jnp.where(lane == gi, ninf, gs)
        allowed = jnp.where(grp == gi, 1.0, allowed)
    ve = jnp.where(allowed > 0, vb, ninf)
    ids = jnp.zeros(scores.shape, jnp.int32)
    wts = jnp.zeros(scores.shape, F32)
    sel = jnp.zeros(scores.shape, F32)
    hits = []
    for k in range(TOP_K):
        _, ei = _first_argmax(ve, lane)
        hit = lane == ei
        hits.append(hit)
        sk = jnp.sum(jnp.where(hit, scores, 0.0), axis=-1, keepdims=True)
        ids = jnp.where(lane == k, ei, ids)
        wts = jnp.where(lane == k, sk, wts)
        sel = jnp.where(hit, 1.0, sel)
        ve = jnp.where(hit, ninf, ve)
    ids_ref[...] = ids
    wts_ref[...] = wts / jnp.sum(wts, axis=-1, keepdims=True) * ROUTE_SCALE

    @pl.when(pl.program_id(0) == 0)
    def _():
        run_ref[...] = cnt0_ref[...]

    before = jnp.dot(tri_ref[...], sel.astype(BF16), preferred_element_type=F32) + run_ref[...]
    ranks = jnp.zeros(scores.shape, F32)
    for k in range(TOP_K):
        rk = jnp.sum(jnp.where(hits[k], before, 0.0), axis=-1, keepdims=True)
        ranks = jnp.where(lane == k, rk, ranks)
    rank_ref[...] = ranks.astype(jnp.int32)
    run_ref[...] += jnp.sum(sel, axis=0, keepdims=True)
    cnt_ref[...] = run_ref[...]


def route(x2d, norm_w, sc1p, sh, router_w, router_bias, rows_per_mod, cnt0, tm=256):
    m, d = x2d.shape
    tm = min(tm, m)
    rw = jnp.pad(router_w, ((0, 0), (0, LANES - N_EXPERTS)))
    rb = jnp.pad(router_bias, (0, LANES - N_EXPERTS)).reshape(1, LANES)
    tri = jnp.asarray(np.tril(np.ones((tm, tm), np.float32), -1), dtype=BF16)
    mod_idx = lambda i: ((i * tm) // rows_per_mod, 0, 0)
    tok_lanes = pl.BlockSpec((tm, LANES), lambda i: (i, 0))
    one_row = pl.BlockSpec((1, LANES), lambda i: (0, 0))
    return pl.pallas_call(
        _route_kernel,
        out_shape=(jax.ShapeDtypeStruct((m * ROW_CHUNKS, LANES), jnp.uint32),
                   jax.ShapeDtypeStruct((m, LANES), jnp.int32),
                   jax.ShapeDtypeStruct((m, LANES), F32), jax.ShapeDtypeStruct((m, LANES), jnp.int32),
                   jax.ShapeDtypeStruct((1, LANES), F32)),
        grid=(m // tm,),
        in_specs=[
            pl.BlockSpec((tm, d), lambda i: (i, 0)),
            pl.BlockSpec((1, d), lambda i: (0, 0)),
            pl.BlockSpec((None, 1, d), mod_idx),
            pl.BlockSpec((None, 1, d), mod_idx),
            pl.BlockSpec((d, LANES), lambda i: (0, 0)),
            one_row,
            pl.BlockSpec((tm, tm), lambda i: (0, 0)),
            one_row,
        ],
        out_specs=(pl.BlockSpec((tm * ROW_CHUNKS, LANES), lambda i: (i, 0)), tok_lanes, tok_lanes, tok_lanes,
                   one_row),
        scratch_shapes=[pltpu.VMEM((1, LANES), F32)],
        compiler_params=_params("arbitrary"),
        name="route",
    )(x2d, norm_w.reshape(1, d), sc1p, sh, rw, rb, tri, cnt0)


def _half_k_dot(lo, hi, w_ref):
    half = lo.shape[1]
    return (jnp.dot(lo, w_ref[0:half, :], preferred_element_type=F32)
            + jnp.dot(hi, w_ref[half:, :], preferred_element_type=F32))


def _swiglu_packed(p, wg_ref, wu_ref, wd_ref):
    lo, hi = (v.astype(BF16) for v in _unpack_halves(p))
    hm = (_silu(_half_k_dot(lo, hi, wg_ref)) * _half_k_dot(lo, hi, wu_ref)).astype(BF16)
    return jnp.dot(hm, wd_ref[...], preferred_element_type=F32)


def _shared_ffn_kernel(h_ref, wg_ref, wu_ref, wd_ref, o_ref, acc_ref, wgb, wub, wdb):
    e = pl.program_id(1)
    wgb[...] = wg_ref[...].astype(BF16)
    wub[...] = wu_ref[...].astype(BF16)
    wdb[...] = wd_ref[...].astype(BF16)
    part = _swiglu_packed(_load_row_blocked(h_ref, acc_ref.shape[0]), wgb, wub, wdb)

    @pl.when(e == 0)
    def _():
        acc_ref[...] = part

    @pl.when(e > 0)
    def _():
        acc_ref[...] += part

    @pl.when(e == pl.num_programs(1) - 1)
    def _():
        o_ref[...] = acc_ref[...].astype(o_ref.dtype)


def shared_expert(h2p, wg, wu, wd, layer, tm=512):
    m = h2p.shape[0] // ROW_CHUNKS
    _, d, f = wg.shape
    tm = min(tm, m)
    fe = EXPERT_DIM
    return pl.pallas_call(
        _shared_ffn_kernel,
        out_shape=jax.ShapeDtypeStruct((m, d), BF16),
        grid=(m // tm, f // fe),
        in_specs=[
            pl.BlockSpec((tm * ROW_CHUNKS, LANES), lambda i, e: (i, 0)),
            pl.BlockSpec((None, d, fe), lambda i, e: (layer, 0, e)),
            pl.BlockSpec((None, d, fe), lambda i, e: (layer, 0, e)),
            pl.BlockSpec((None, fe, d), lambda i, e: (layer, e, 0)),
        ],
        out_specs=pl.BlockSpec((tm, d), lambda i, e: (i, 0)),
        scratch_shapes=[pltpu.VMEM((tm, d), F32), pltpu.VMEM((d, fe), BF16), pltpu.VMEM((d, fe), BF16),
                        pltpu.VMEM((fe, d), BF16)],
        compiler_params=_params("parallel", "arbitrary"),
        name="shared_expert",
    )(h2p, wg, wu, wd)


EXPERT_TILE = 256
COMBINE_TILE = 64


def dispatch_plan(ids, ranks, counts):
    n = ids.shape[0]
    a = n * TOP_K
    tm = EXPERT_TILE
    n_tiles = a // tm + N_EXPERTS
    tiles_per = (counts + tm - 1) // tm
    tile_end = jnp.cumsum(tiles_per)
    row_start = (tile_end - tiles_per) * tm
    experts = jnp.arange(N_EXPERTS, dtype=jnp.int32)
    row_of = ranks + jnp.sum(jnp.where(ids[..., None] == experts, row_start, 0), axis=-1)
    tok = jnp.arange(a, dtype=jnp.int32) // TOP_K
    row_tok = jnp.zeros((n_tiles * tm,), jnp.int32).at[row_of.reshape(a)].set(tok)
    n_used = tile_end[-1:]
    tiles = jnp.minimum(jnp.arange(n_tiles, dtype=jnp.int32), n_used[0] - 1)
    tile_e = jnp.sum((tile_end[None, :] <= tiles[:, None]).astype(jnp.int32), axis=1)
    tile_e = jnp.minimum(tile_e, N_EXPERTS - 1)
    return row_tok.reshape(n_tiles, 1, tm), tile_e, n_used.astype(jnp.int32), row_of.reshape(n, TOP_K)


def _row_block(r):
    return pl.ds(pl.multiple_of(r * ROW_CHUNKS, ROW_CHUNKS), ROW_CHUNKS)


def _expert_kernel(te_ref, nu_ref, cur_ref, nxt_ref, h_ref, wg_ref, wu_ref, wd_ref, o_ref,
                   xbuf, wgb, wub, wdb, sem):
    j = pl.program_id(0)
    n_used = nu_ref[0]
    slot = j % 2
    tm = xbuf.shape[1] // ROW_CHUNKS

    def gather(idx_ref, s):
        def body(r, carry):
            t = idx_ref[0, r]
            pltpu.make_async_copy(h_ref.at[_row_block(t), :], xbuf.at[s, _row_block(r), :], sem.at[s]).start()
            return carry
        lax.fori_loop(0, tm, body, 0, unroll=8)

    @pl.when(j == 0)
    def _():
        gather(cur_ref, 0)

    @pl.when(j + 1 < n_used)
    def _():
        gather(nxt_ref, 1 - slot)

    @pl.when(j < n_used)
    def _():
        @pl.when((j == 0) | (te_ref[j] != te_ref[jnp.maximum(j - 1, 0)]))
        def _():
            wgb[...] = wg_ref[...].astype(BF16)
            wub[...] = wu_ref[...].astype(BF16)
            wdb[...] = wd_ref[...].astype(BF16)

        pltpu.make_async_copy(xbuf.at[slot], xbuf.at[slot], sem.at[slot]).wait()
        x = _load_row_blocked(xbuf.at[slot], tm)
        _store_row_blocked(o_ref, _pack_halves(_swiglu_packed(x, wgb, wub, wdb)))

    @pl.when(j >= n_used)
    def _():
        o_ref[...] = jnp.zeros(o_ref.shape, o_ref.dtype)


def routed_experts(h2p, row_tok, tile_e, n_used, wg, wu, wd, layer):
    n_tiles, _, tm = row_tok.shape
    _, ne, d, f = wg.shape
    assert d == 2 * ROW_CHUNKS * LANES
    last = n_tiles - 1
    idx_spec = lambda shift: pl.BlockSpec((None, 1, tm), lambda j, te, nu: (jnp.minimum(j + shift, last), 0, 0),
                                          memory_space=pltpu.SMEM)
    return pl.pallas_call(
        _expert_kernel,
        out_shape=jax.ShapeDtypeStruct((n_tiles * tm * ROW_CHUNKS, LANES), jnp.uint32),
        grid_spec=pltpu.PrefetchScalarGridSpec(
            num_scalar_prefetch=2,
            grid=(n_tiles,),
            in_specs=[
                idx_spec(0), idx_spec(1),
                pl.BlockSpec(memory_space=pl.ANY),
                pl.BlockSpec((None, None, d, f), lambda j, te, nu: (layer, te[j], 0, 0)),
                pl.BlockSpec((None, None, d, f), lambda j, te, nu: (layer, te[j], 0, 0)),
                pl.BlockSpec((None, None, f, d), lambda j, te, nu: (layer, te[j], 0, 0)),
            ],
            out_specs=pl.BlockSpec((tm * ROW_CHUNKS, LANES), lambda j, te, nu: (j, 0)),
            scratch_shapes=[pltpu.VMEM((2, tm * ROW_CHUNKS, LANES), jnp.uint32), pltpu.VMEM((d, f), BF16),
                            pltpu.VMEM((d, f), BF16), pltpu.VMEM((f, d), BF16), pltpu.SemaphoreType.DMA((2,))],
        ),
        compiler_params=_params("arbitrary"),
        name="routed_experts",
    )(tile_e, n_used, row_tok, row_tok, h2p, wg, wu, wd)


def _combine_kernel(cur_ref, nxt_ref, ys_ref, w_ref, ysh_ref, x_ref, g_ref, nf_ref, o_ref, gbuf, sem, *,
                    final_norm):
    i = pl.program_id(0)
    slot = i % 2
    tmc = x_ref.shape[0]

    def gather(idx_ref, s):
        def body(r, carry):
            dst_rows = pl.ds(pl.multiple_of(r * ROW_CHUNKS, ROW_CHUNKS), ROW_CHUNKS)
            for k in range(TOP_K):
                t = idx_ref[0, r * TOP_K + k]
                src_rows = pl.ds(pl.multiple_of(t * ROW_CHUNKS, ROW_CHUNKS), ROW_CHUNKS)
                pltpu.make_async_copy(ys_ref.at[src_rows, :], gbuf.at[s, k, dst_rows, :], sem.at[s]).start()
            return carry
        lax.fori_loop(0, tmc, body, 0)

    @pl.when(i == 0)
    def _():
        gather(cur_ref, 0)

    @pl.when(i + 1 < pl.num_programs(0))
    def _():
        gather(nxt_ref, 1 - slot)

    pltpu.make_async_copy(gbuf.at[slot], gbuf.at[slot], sem.at[slot]).wait()
    w = w_ref[...]
    wk = [w[:, k:k + 1] for k in range(TOP_K)]
    cols_lo, cols_hi = [], []
    for c in range(ROW_CHUNKS):
        acc_lo = acc_hi = None
        for k in range(TOP_K):
            lo, hi = _unpack_halves(gbuf[slot, k, pl.ds(c, tmc, stride=ROW_CHUNKS), :])
            acc_lo = wk[k] * lo if acc_lo is None else acc_lo + wk[k] * lo
            acc_hi = wk[k] * hi if acc_hi is None else acc_hi + wk[k] * hi
        cols_lo.append(acc_lo)
        cols_hi.append(acc_hi)
    y = jnp.concatenate(cols_lo + cols_hi, axis=1) + ysh_ref[...].astype(F32)
    x = x_ref[...] + g_ref[...] * y
    if final_norm:
        x = x * lax.rsqrt(jnp.mean(x * x, axis=-1, keepdims=True) + EPS) * nf_ref[...]
    o_ref[...] = x


def combine_residual(x2d, ysh, ys, row_of, wts, gate, norm_final, rows_per_mod, row0, final_norm):
    m, d = x2d.shape
    tmc = COMBINE_TILE
    n_steps = m // tmc
    b0 = row0 // tmc
    idx = row_of.reshape(-1, 1, tmc * TOP_K)
    idx_spec = lambda shift: pl.BlockSpec(
        (None, 1, tmc * TOP_K), lambda i: (b0 + jnp.minimum(i + shift, n_steps - 1), 0, 0), memory_space=pltpu.SMEM)
    return pl.pallas_call(
        functools.partial(_combine_kernel, final_norm=final_norm),
        out_shape=jax.ShapeDtypeStruct((m, d), F32),
        grid=(n_steps,),
        in_specs=[
            idx_spec(0), idx_spec(1),
            pl.BlockSpec(memory_space=pl.ANY),
            pl.BlockSpec((tmc, LANES), lambda i: (b0 + i, 0)),
            pl.BlockSpec((tmc, d), lambda i: (b0 + i, 0)),
            pl.BlockSpec((tmc, d), lambda i: (i, 0)),
            pl.BlockSpec((None, 1, d), lambda i: ((i * tmc) // rows_per_mod, 0, 0)),
            pl.BlockSpec((1, d), lambda i: (0, 0)),
        ],
        out_specs=pl.BlockSpec((tmc, d), lambda i: (i, 0)),
        scratch_shapes=[pltpu.VMEM((2, TOP_K, tmc * ROW_CHUNKS, LANES), jnp.uint32),
                        pltpu.SemaphoreType.DMA((2,))],
        compiler_params=_params("arbitrary"),
        name="combine_residual",
    )(idx, idx, ys, wts, ysh, x2d, gate, norm_final.reshape(1, d))


def kernel(x, c, ctx, c_ctx, ada_w, ada_b, norm_mix, norm_ffn, w_in, dn_conv, dn_a_log, dn_dt_bias, dn_norm, attn_sink, pool_w, pool_scale, w_branch, w_out, router_w, router_bias, exp_gate, exp_up, exp_down, shared_gate, shared_up, shared_down, norm_final):
    batch, seq_len, d = x.shape
    n_ctx = ctx.shape[1]
    depth = ada_w.shape[0]
    cond_rows = 16
    cond = jnp.zeros((cond_rows, d), F32).at[:batch].set(c).at[batch].set(c_ctx)
    mod_all = ada_modulation(cond, ada_w, ada_b).reshape(depth, cond_rows, 6, 1, d)
    cos_t, sin_t = rope_tables(seq_len)
    xl = x.reshape(batch * seq_len, d)
    xc = ctx.reshape(batch * n_ctx, d)
    zero_state = jnp.zeros((batch, DN_HEADS, 2, HEAD_DIM, HEAD_DIM), F32)
    for i in range(depth):
        last = i == depth - 1
        sh1, sc1, g1, sh2, sc2, g2 = (mod_all[i, :, s] for s in range(6))
        lat = lambda m: m[:batch]
        cx = lambda m: m[batch:batch + 1]
        w_perm = permute_projection_columns(w_in[i].astype(BF16))
        wb = w_branch[i].astype(BF16)
        wo = w_out[i].astype(BF16)
        w_ctx = w_perm[:, :COLS_CTX] if last else w_perm
        p_ctx = norm_project(xc, norm_mix[i], 1.0 + cx(sc1), cx(sh1), w_ctx, batch * n_ctx)
        p_lat = norm_project(xl, norm_mix[i], 1.0 + lat(sc1), lat(sh1), w_perm, seq_len)

        y_dn_c, s_ctx = deltanet_mix(p_ctx, dn_a_log[i], dn_dt_bias[i], dn_conv[i], dn_norm[i], zero_state,
                                     batch, n_ctx, with_q=not last)
        if not last:
            branches_c = (fourier_mix(p_ctx, batch, n_ctx), y_dn_c,
                          pool_mix(p_ctx, pool_w[i], pool_scale[i], batch, n_ctx),
                          context_attention(p_ctx, attn_sink[i], batch, n_ctx))
            xc = out_project_residual(merge_gate(branches_c, p_ctx, wb), wo, xc, cx(g1), batch * n_ctx)

        y_dn, _ = deltanet_mix(p_lat, dn_a_log[i], dn_dt_bias[i], dn_conv[i], dn_norm[i], s_ctx, batch, seq_len)
        branches = (fourier_mix(p_lat, batch, seq_len), y_dn,
                    pool_mix(p_lat, pool_w[i], pool_scale[i], batch, seq_len),
                    latent_attention(p_lat, p_ctx, attn_sink[i], cos_t, sin_t, batch, seq_len, n_ctx))
        xl = out_project_residual(merge_gate(branches, p_lat, wb), wo, xl, lat(g1), seq_len)

        no_tokens = jnp.zeros((1, LANES), F32)
        h2p, ids, wts, ranks, counts = route(xl, norm_ffn[i], 1.0 + lat(sc2), lat(sh2), router_w[i],
                                             router_bias[i], seq_len, no_tokens)
        if not last:
            *routed_c, counts = route(xc, norm_ffn[i], 1.0 + cx(sc2), cx(sh2), router_w[i], router_bias[i],
                                      batch * n_ctx, counts)
            h2p, ids, wts, ranks = (jnp.concatenate(pair, axis=0)
                                    for pair in zip((h2p, ids, wts, ranks), routed_c))
        row_tok, tile_e, n_used, row_of = dispatch_plan(ids[:, :TOP_K], ranks[:, :TOP_K],
                                                        counts[0, :N_EXPERTS].astype(jnp.int32))
        ys = routed_experts(h2p, row_tok, tile_e, n_used, exp_gate, exp_up, exp_down, i)
        ysh = shared_expert(h2p, shared_gate, shared_up, shared_down, i)
        n_lat = batch * seq_len
        if not last:
            xc = combine_residual(xc, ysh, ys, row_of, wts, cx(g2), norm_final, batch * n_ctx, n_lat, False)
        xl = combine_residual(xl, ysh, ys, row_of, wts, lat(g2), norm_final, seq_len, 0, last)
    return xl.reshape(batch, seq_len, d)
```

```python
import functools
import math

import jax
import jax.numpy as jnp
import numpy as np
from jax import lax
from jax.experimental import pallas as pl
from jax.experimental.pallas import tpu as pltpu

F32 = jnp.float32
BF16 = jnp.bfloat16

HEAD_DIM = 128
GRID_W = 64
ROPE_BASE = 10000.0
EPS = 1e-6
NEG_INF = -1e30
N_BRANCHES = 4
BRANCH_WIDTH = 1024
ATT_Q_HEADS = 8
ATT_KV_HEADS = 2
ATT_GROUP = ATT_Q_HEADS // ATT_KV_HEADS
WINDOW = 128
ATT_BLOCK = 128
DN_HEADS = 8
DN_WIDTH = DN_HEADS * HEAD_DIM
DN_CONV = 5
DN_CHUNK = 64
DN_SUPER = 256
FT_GROUPS = 4
FT_GROUP_DIM = BRANCH_WIDTH // FT_GROUPS
POOL_WINDOWS = (2, 4, 8, 16)
POOL_GROUP_DIM = BRANCH_WIDTH // len(POOL_WINDOWS)
N_EXPERTS = 64
N_EXPERT_GROUPS = 8
TOPK_GROUPS = 4
TOP_K = 8
EXPERT_DIM = 256
ROUTE_SCALE = 2.5

VMEM_LIMIT_BYTES = 56 * 1024 * 1024
LANES = 128

ATT_KV_W = ATT_KV_HEADS * HEAD_DIM
ATT_Q_W = ATT_Q_HEADS * HEAD_DIM
GATE_W = 16384
COL_KA = 0
COL_VA = COL_KA + ATT_KV_W
COL_KD = COL_VA + ATT_KV_W
COL_VD = COL_KD + DN_WIDTH
COL_AB = COL_VD + DN_WIDTH
COLS_CTX_SRC = COL_AB + 4 * DN_HEADS
COLS_CTX = 3072
CTX_PAD = COLS_CTX - COLS_CTX_SRC
COL_QA = COLS_CTX
COL_QD = COL_QA + ATT_Q_W
COL_ZD = COL_QD + DN_WIDTH
COL_FT = COL_ZD + DN_WIDTH
COL_POOL = COL_FT + BRANCH_WIDTH
COL_GATE = COL_POOL + BRANCH_WIDTH
COLS_PAD = COL_GATE + GATE_W


def _params(*sem):
    return pltpu.CompilerParams(dimension_semantics=sem, vmem_limit_bytes=VMEM_LIMIT_BYTES)


def _silu(x):
    return x * jax.nn.sigmoid(x)


def _ada_kernel(c_ref, w_ref, b_ref, o_ref):
    s = _silu(c_ref[...]).astype(BF16)
    o_ref[...] = jnp.dot(s, w_ref[...].astype(BF16), preferred_element_type=F32) + b_ref[...]


def ada_modulation(cond, ada_w, ada_b, tn=512):
    depth, d, n = ada_w.shape
    rows = cond.shape[0]
    return pl.pallas_call(
        _ada_kernel,
        out_shape=jax.ShapeDtypeStruct((depth, rows, n), F32),
        grid=(depth, n // tn),
        in_specs=[
            pl.BlockSpec((rows, d), lambda l, j: (0, 0)),
            pl.BlockSpec((None, d, tn), lambda l, j: (l, 0, j)),
            pl.BlockSpec((None, 1, tn), lambda l, j: (l, 0, j)),
        ],
        out_specs=pl.BlockSpec((None, rows, tn), lambda l, j: (l, 0, j)),
        compiler_params=_params("parallel", "parallel"),
        name="ada_modulation",
    )(cond, ada_w, ada_b.reshape(depth, 1, n))


def _modulated_norm(x, g, sc1p, sh):
    y = x * lax.rsqrt(jnp.mean(x * x, axis=-1, keepdims=True) + EPS) * g
    return y * sc1p + sh


def _norm_kernel(x_ref, g_ref, sc_ref, sh_ref, o_ref):
    o_ref[...] = _modulated_norm(x_ref[...], g_ref[...], sc_ref[...], sh_ref[...]).astype(o_ref.dtype)


def _matmul_kernel(a_ref, w_ref, o_ref):
    o_ref[...] = jnp.dot(a_ref[...], w_ref[...], preferred_element_type=F32).astype(o_ref.dtype)


def norm_project(x2d, norm_w, sc1p, sh, w, layer, n, rows_per_mod, tm=1024, tn=1024, out_dtype=F32):
    m, d = x2d.shape
    tr = min(256, m)
    mod_idx = lambda i: ((i * tr) // rows_per_mod, 0, 0)
    hn = pl.pallas_call(
        _norm_kernel,
        out_shape=jax.ShapeDtypeStruct((m, d), BF16),
        grid=(m // tr,),
        in_specs=[
            pl.BlockSpec((tr, d), lambda i: (i, 0)),
            pl.BlockSpec((1, d), lambda i: (0, 0)),
            pl.BlockSpec((None, 1, d), mod_idx),
            pl.BlockSpec((None, 1, d), mod_idx),
        ],
        out_specs=pl.BlockSpec((tr, d), lambda i: (i, 0)),
        compiler_params=_params("parallel"),
        name="modulated_norm",
    )(x2d, norm_w.reshape(1, d), sc1p, sh)
    tm = min(tm, m)
    return pl.pallas_call(
        _matmul_kernel,
        out_shape=jax.ShapeDtypeStruct((m, n), out_dtype),
        grid=(m // tm, n // tn),
        in_specs=[
            pl.BlockSpec((tm, d), lambda i, j: (i, 0)),
            pl.BlockSpec((None, d, tn), lambda i, j: (layer, 0, j)),
        ],
        out_specs=pl.BlockSpec((tm, tn), lambda i, j: (i, j)),
        compiler_params=_params("parallel", "arbitrary"),
        name="project",
    )(hn, w)


def _rope(x, cos, sin_signed):
    lane = lax.broadcasted_iota(jnp.int32, x.shape, 1)
    swapped = jnp.where((lane % 64) < 32, pltpu.roll(x, 96, 1), pltpu.roll(x, 32, 1))
    return x * cos + swapped * sin_signed


def _attn_kernel(*refs, banded, n_ctx, seq_len):
    if banded:
        (sink_ref, q_ref, kp_ref, kc_ref, kn_ref, vp_ref, vc_ref, vn_ref, kx_ref, vx_ref,
         cos_ref, sin_ref, o_ref) = refs
    else:
        sink_ref, q_ref, kx_ref, vx_ref, o_ref = refs
    blk = q_ref.shape[0]
    scale = HEAD_DIM ** -0.5
    n = pl.program_id(1)
    if banded:
        def table(ref, blk_idx):
            start = pl.multiple_of(blk_idx * blk, blk)
            return ref[pl.ds(start, blk), :]
        nblocks = seq_len // blk
        ip = jnp.maximum(n - 1, 0)
        inx = jnp.minimum(n + 1, nblocks - 1)
        cos_q, sin_q = table(cos_ref, n), table(sin_ref, n)
        cos_p, sin_p = table(cos_ref, ip), table(sin_ref, ip)
        cos_n, sin_n = table(cos_ref, inx), table(sin_ref, inx)
        rows = lax.broadcasted_iota(jnp.int32, (ATT_GROUP * blk, 3 * blk), 0) % blk
        cols = lax.broadcasted_iota(jnp.int32, (ATT_GROUP * blk, 3 * blk), 1)
        rel = cols - blk - rows
        kpos = (n - 1) * blk + cols
        valid = (jnp.abs(rel) <= WINDOW) & (kpos >= 0) & (kpos < seq_len)
    for h in range(ATT_KV_HEADS):
        hs = slice(h * HEAD_DIM, (h + 1) * HEAD_DIM)
        qs, sinks = [], []
        for g in range(ATT_GROUP):
            hq = h * ATT_GROUP + g
            qh = q_ref[:, hq * HEAD_DIM:(hq + 1) * HEAD_DIM]
            if banded:
                qh = _rope(qh, cos_q, sin_q)
            qs.append((qh * scale).astype(BF16))
            sinks.append(jnp.full((blk, 1), sink_ref[hq], F32))
        q4 = jnp.concatenate(qs, axis=0)
        sink = jnp.concatenate(sinks, axis=0)
        kx = kx_ref[:, hs].astype(BF16)
        vx = vx_ref[:, hs].astype(BF16)
        s_ctx = lax.dot_general(q4, kx, (((1,), (1,)), ((), ())), preferred_element_type=F32)
        m = jnp.maximum(jnp.max(s_ctx, axis=-1, keepdims=True), sink)
        if banded:
            kband = jnp.concatenate([
                _rope(kp_ref[:, hs], cos_p, sin_p), _rope(kc_ref[:, hs], cos_q, sin_q),
                _rope(kn_ref[:, hs], cos_n, sin_n)], axis=0).astype(BF16)
            vband = jnp.concatenate([vp_ref[:, hs], vc_ref[:, hs], vn_ref[:, hs]], axis=0).astype(BF16)
            s_loc = lax.dot_general(q4, kband, (((1,), (1,)), ((), ())), preferred_element_type=F32)
            s_loc = jnp.where(valid, s_loc, NEG_INF)
            m = jnp.maximum(m, jnp.max(s_loc, axis=-1, keepdims=True))
        p_ctx = jnp.exp(s_ctx - m)
        denom = jnp.sum(p_ctx, axis=-1, keepdims=True) + jnp.exp(sink - m)
        o = jnp.dot(p_ctx.astype(BF16), vx, preferred_element_type=F32)
        if banded:
            p_loc = jnp.exp(s_loc - m)
            denom = denom + jnp.sum(p_loc, axis=-1, keepdims=True)
            o = o + jnp.dot(p_loc.astype(BF16), vband, preferred_element_type=F32)
        o = o / denom
        for g in range(ATT_GROUP):
            hq = h * ATT_GROUP + g
            o_ref[:, hq * HEAD_DIM:(hq + 1) * HEAD_DIM] = o[g * blk:(g + 1) * blk].astype(o_ref.dtype)


def latent_attention(p_lat, p_ctx, sink, cos_t, sin_t, batch, seq_len, n_ctx):
    blk = ATT_BLOCK
    nb = seq_len // blk
    cq = COL_QA // ATT_Q_W
    ck, cv = COL_KA // ATT_KV_W, COL_VA // ATT_KV_W
    xk, xv = ck, cv
    row = lambda b, n: b * nb + n
    kspec = lambda col, shift: pl.BlockSpec(
        (blk, ATT_KV_W), lambda b, n: (row(b, jnp.clip(n + shift, 0, nb - 1)), col))
    return pl.pallas_call(
        functools.partial(_attn_kernel, banded=True, n_ctx=n_ctx, seq_len=seq_len),
        out_shape=jax.ShapeDtypeStruct((batch * seq_len, ATT_Q_W), BF16),
        grid=(batch, nb),
        in_specs=[
            pl.BlockSpec(memory_space=pltpu.SMEM),
            pl.BlockSpec((blk, ATT_Q_W), lambda b, n: (row(b, n), cq)),
            kspec(ck, -1), kspec(ck, 0), kspec(ck, 1),
            kspec(cv, -1), kspec(cv, 0), kspec(cv, 1),
            pl.BlockSpec((n_ctx, ATT_KV_W), lambda b, n: (b, xk)),
            pl.BlockSpec((n_ctx, ATT_KV_W), lambda b, n: (b, xv)),
            pl.BlockSpec((seq_len, HEAD_DIM), lambda b, n: (0, 0)),
            pl.BlockSpec((seq_len, HEAD_DIM), lambda b, n: (0, 0)),
        ],
        out_specs=pl.BlockSpec((blk, ATT_Q_W), lambda b, n: (row(b, n), 0)),
        compiler_params=_params("parallel", "arbitrary"),
        name="latent_attention",
    )(sink, p_lat, p_lat, p_lat, p_lat, p_lat, p_lat, p_lat, p_ctx, p_ctx, cos_t, sin_t)


def context_attention(p_ctx, sink, batch, n_ctx):
    cq = COL_QA // ATT_Q_W
    ck, cv = COL_KA // ATT_KV_W, COL_VA // ATT_KV_W
    return pl.pallas_call(
        functools.partial(_attn_kernel, banded=False, n_ctx=n_ctx, seq_len=n_ctx),
        out_shape=jax.ShapeDtypeStruct((batch * n_ctx, ATT_Q_W), BF16),
        grid=(batch, 1),
        in_specs=[
            pl.BlockSpec(memory_space=pltpu.SMEM),
            pl.BlockSpec((n_ctx, ATT_Q_W), lambda b, n: (b, cq)),
            pl.BlockSpec((n_ctx, ATT_KV_W), lambda b, n: (b, ck)),
            pl.BlockSpec((n_ctx, ATT_KV_W), lambda b, n: (b, cv)),
        ],
        out_specs=pl.BlockSpec((n_ctx, ATT_Q_W), lambda b, n: (b, 0)),
        compiler_params=_params("parallel", "arbitrary"),
        name="context_attention",
    )(sink, p_ctx, p_ctx, p_ctx)


def rope_tables(seq_len):
    rows = seq_len // GRID_W
    row = jnp.repeat(jnp.arange(rows, dtype=F32), GRID_W)
    col = jnp.tile(jnp.arange(GRID_W, dtype=F32), rows)
    half = HEAD_DIM // 2
    inv_freq = ROPE_BASE ** (-jnp.arange(0, half, 2, dtype=F32) / half)
    ang_r, ang_c = row[:, None] * inv_freq, col[:, None] * inv_freq
    cos_t = jnp.concatenate([jnp.cos(ang_r)] * 2 + [jnp.cos(ang_c)] * 2, axis=-1)
    sin_t = jnp.concatenate([-jnp.sin(ang_r), jnp.sin(ang_r), -jnp.sin(ang_c), jnp.sin(ang_c)], axis=-1)
    return cos_t, sin_t


def _fourier_kernel(u_ref, cc_ref, sc_ref, ct_ref, st_ref, o_ref, p_ref, q_ref, *, scale):
    @pl.when(pl.program_id(2) == 0)
    def _():
        u = u_ref[...].astype(BF16)
        p_ref[...] = jnp.dot(u, cc_ref[...], preferred_element_type=F32).astype(BF16)
        q_ref[...] = jnp.dot(u, sc_ref[...], preferred_element_type=F32).astype(BF16)

    re = (jnp.dot(ct_ref[...], p_ref[...], preferred_element_type=F32)
          - jnp.dot(st_ref[...], q_ref[...], preferred_element_type=F32))
    o_ref[...] = (re * scale).astype(o_ref.dtype)


def dft_tables(n):
    k = jnp.arange(n, dtype=jnp.int32)
    ang = ((k[:, None] * k[None, :]) % n).astype(F32) * (2.0 * math.pi / n)
    return jnp.cos(ang).astype(BF16), jnp.sin(ang).astype(BF16)


def fourier_mix(p2d, batch, seq_len, tr=512):
    tr = min(tr, seq_len)
    cc, sc = dft_tables(FT_GROUP_DIM)
    ct, st = dft_tables(seq_len)
    c0 = COL_FT // FT_GROUP_DIM
    nr = seq_len // tr
    return pl.pallas_call(
        functools.partial(_fourier_kernel, scale=1.0 / math.sqrt(seq_len * FT_GROUP_DIM)),
        out_shape=jax.ShapeDtypeStruct((batch * seq_len, BRANCH_WIDTH), BF16),
        grid=(batch, FT_GROUPS, nr),
        in_specs=[
            pl.BlockSpec((seq_len, FT_GROUP_DIM), lambda b, g, r: (b, c0 + g)),
            pl.BlockSpec((FT_GROUP_DIM, FT_GROUP_DIM), lambda b, g, r: (0, 0)),
            pl.BlockSpec((FT_GROUP_DIM, FT_GROUP_DIM), lambda b, g, r: (0, 0)),
            pl.BlockSpec((tr, seq_len), lambda b, g, r: (r, 0)),
            pl.BlockSpec((tr, seq_len), lambda b, g, r: (r, 0)),
        ],
        out_specs=pl.BlockSpec((tr, FT_GROUP_DIM), lambda b, g, r: (b * nr + r, g)),
        scratch_shapes=[pltpu.VMEM((seq_len, FT_GROUP_DIM), BF16)] * 2,
        compiler_params=_params("parallel", "parallel", "arbitrary"),
        name="fourier_mix",
    )(p2d, cc, sc, ct, st)


POOL_TILE = 256
POOL_HALO = 128


def _pool_kernel(u_ref, band_ref, w_ref, s_ref, o_ref, pad_ref, *, seq_len):
    g = pl.program_id(1)
    zeros = jnp.zeros((POOL_HALO, POOL_GROUP_DIM), BF16)
    pad_ref[0:POOL_HALO, :] = zeros
    pad_ref[POOL_HALO + seq_len:POOL_HALO + seq_len + POOL_HALO, :] = zeros
    pad_ref[POOL_HALO:POOL_HALO + seq_len, :] = u_ref[...].astype(BF16)
    half = jnp.left_shift(1, g)
    for t in range(seq_len // POOL_TILE):
        r0 = t * POOL_TILE
        win = pad_ref[r0:r0 + POOL_TILE + 2 * POOL_HALO, :]
        sums = jnp.dot(band_ref[...], win, preferred_element_type=F32)
        pos = r0 + lax.broadcasted_iota(jnp.int32, (POOL_TILE, 1), 0)
        cnt = jnp.minimum(pos + half, seq_len) - jnp.maximum(pos - half, 0)
        m = sums / cnt.astype(F32) - u_ref[r0:r0 + POOL_TILE, :]
        y = jnp.dot(m.astype(BF16), w_ref[...], preferred_element_type=F32) * s_ref[...]
        o_ref[r0:r0 + POOL_TILE, :] = y.astype(o_ref.dtype)


def pool_bands():
    r = np.arange(POOL_TILE)[:, None]
    c = np.arange(POOL_TILE + 2 * POOL_HALO)[None, :]
    rel = c - POOL_HALO - r
    bands = [((rel >= -(w // 2)) & (rel <= w // 2 - 1)) for w in POOL_WINDOWS]
    return jnp.asarray(np.stack(bands).astype(np.float32), dtype=BF16)


def pool_mix(p2d, pool_w, pool_scale, batch, seq_len):
    c0 = COL_POOL // POOL_GROUP_DIM
    ng = len(POOL_WINDOWS)
    return pl.pallas_call(
        functools.partial(_pool_kernel, seq_len=seq_len),
        out_shape=jax.ShapeDtypeStruct((batch * seq_len, BRANCH_WIDTH), BF16),
        grid=(batch, ng),
        in_specs=[
            pl.BlockSpec((seq_len, POOL_GROUP_DIM), lambda b, g: (b, c0 + g)),
            pl.BlockSpec((None, POOL_TILE, POOL_TILE + 2 * POOL_HALO), lambda b, g: (g, 0, 0)),
            pl.BlockSpec((None, POOL_GROUP_DIM, POOL_GROUP_DIM), lambda b, g: (g, 0, 0)),
            pl.BlockSpec((1, POOL_GROUP_DIM), lambda b, g: (0, g)),
        ],
        out_specs=pl.BlockSpec((seq_len, POOL_GROUP_DIM), lambda b, g: (b, g)),
        scratch_shapes=[pltpu.VMEM((seq_len + 2 * POOL_HALO, POOL_GROUP_DIM), BF16)],
        compiler_params=_params("parallel", "parallel"),
        name="pool_mix",
    )(p2d, pool_bands(), pool_w.astype(BF16), pool_scale.reshape(1, BRANCH_WIDTH))


def permute_projection_columns(w):
    pad = jnp.zeros(w.shape[:-1] + (CTX_PAD,), w.dtype)
    return jnp.concatenate([w[..., :COLS_CTX_SRC], pad, w[..., COLS_CTX_SRC:]], axis=-1)


CONV_PAD = 8
N_LEVELS = 6


def dn_masks():
    i = np.arange(DN_SUPER)[:, None]
    j = np.arange(DN_SUPER)[None, :]
    incl, lev = [], []
    for later in (lambda a, b: a >= b, lambda a, b: a <= b):
        strict = later(i, j) & (i != j)
        incl.append((i // DN_CHUNK == j // DN_CHUNK) & later(i, j))
        lv = [(i // 2 == j // 2) & strict]
        sz = 2
        while sz < DN_CHUNK:
            lv.append((i // (2 * sz) == j // (2 * sz)) & (i // sz != j // sz) & strict)
            sz *= 2
        lev.append(np.stack(lv))
    return (jnp.asarray(np.stack(incl).astype(np.float32)), jnp.asarray(np.stack(lev).astype(np.float32)))


def _split3(x):
    x1 = x.astype(BF16)
    r1 = x - x1.astype(F32)
    x2 = r1.astype(BF16)
    x3 = (r1 - x2.astype(F32)).astype(BF16)
    return x1, x2, x3


def _softplus(x):
    return jnp.maximum(x, 0.0) + jnp.log1p(jnp.exp(-jnp.abs(x)))


def _dn_kernel(alog_ref, dtb_ref, q_ref, k_ref, v_ref, z_ref, ab_ref, cq_ref, ck_ref, cv_ref, nw_ref,
               s0_ref, incl_ref, lev_ref, o_ref, sfin_ref,
               pq_ref, pk_ref, pv_ref, qe_ref, g_ref, b_ref, eg_ref, oacc_ref, *, seq_len, with_q):
    h = pl.program_id(1)
    n_super = seq_len // DN_SUPER
    n_chunks = seq_len // DN_CHUNK
    per_super = DN_SUPER // DN_CHUNK
    zpad = jnp.zeros((CONV_PAD, HEAD_DIM), F32)
    streams = [(pk_ref, k_ref), (pv_ref, v_ref)] + ([(pq_ref, q_ref)] if with_q else [])
    for pad_ref, src_ref in streams:
        pad_ref[0:CONV_PAD, :] = zpad
        pad_ref[CONV_PAD + seq_len:2 * CONV_PAD + seq_len, :] = zpad
        pad_ref[CONV_PAD:CONV_PAD + seq_len, :] = src_ref[...]

    lane = lax.broadcasted_iota(jnp.int32, (1, HEAD_DIM), 1)
    pick = lambda ref: jnp.where(lane == 0, ref[0, h], jnp.where(lane == 1, ref[1, h], 0.0))
    neg_a = -jnp.exp(pick(alog_ref))
    dtb = pick(dtb_ref)
    eye = (lax.broadcasted_iota(jnp.int32, (DN_SUPER, DN_SUPER), 0)
           == lax.broadcasted_iota(jnp.int32, (DN_SUPER, DN_SUPER), 1)).astype(F32)
    half = DN_CONV // 2

    def conv_silu(pad_ref, cw_ref, r0):
        win = pad_ref[pl.ds(r0, DN_SUPER + 2 * CONV_PAD), :]
        acc = None
        for j in range(DN_CONV):
            off = CONV_PAD - half + j
            term = cw_ref[j:j + 1, :] * win[off:off + DN_SUPER, :]
            acc = term if acc is None else acc + term
        return _silu(acc)

    def l2n(x):
        return x * lax.rsqrt(jnp.sum(x * x, axis=-1, keepdims=True) + EPS)

    def chain_setup(sidx):
        r0 = pl.multiple_of(sidx * DN_SUPER, DN_SUPER)
        k = l2n(conv_silu(pk_ref, ck_ref, r0))
        v = conv_silu(pv_ref, cv_ref, r0)
        kb = k.astype(BF16)
        kk = lax.dot_general(kb, kb, (((1,), (1,)), ((), ())), preferred_element_type=F32)
        q = qk = None
        if with_q:
            q = l2n(conv_silu(pq_ref, cq_ref, r0)) * (HEAD_DIM ** -0.5)
            qk = lax.dot_general(q.astype(BF16), kb, (((1,), (1,)), ((), ())), preferred_element_type=F32)
        ab = ab_ref[pl.ds(r0, DN_SUPER), :]
        gt = neg_a * _softplus(ab + dtb)
        bt = jax.nn.sigmoid(ab)
        gparts = _split3(gt)
        gcs = []
        for d in range(2):
            seg = incl_ref[d].astype(BF16)
            gcs.append(sum(jnp.dot(seg, gp, preferred_element_type=F32) for gp in gparts))
        gtot_t = gcs[0] + gcs[1] - gt
        chains = []
        for d in range(2):
            gcol = gcs[d][:, d:d + 1]
            grow = jnp.transpose(gcs[d])[d:d + 1, :]
            beta = bt[:, 2 + d:3 + d]
            e = jnp.exp((gcol - grow) * incl_ref[d])
            nf = (beta * kk) * e
            chains.append(dict(d=d, sidx=sidx, r0=r0, k=k, v=v, q=q, qk=qk, gcol=gcol, gtot=gtot_t[:, d:d + 1],
                               beta=beta, e=e, nf=nf, t=eye - nf * lev_ref[d, 0]))
        return chains

    def chain_finish(ch):
        d, k, gcol, gtot, beta = ch["d"], ch["k"], ch["gcol"], ch["gtot"], ch["beta"]
        egc = jnp.exp(gcol)
        rhs = jnp.concatenate([beta * ch["v"], (beta * egc) * k], axis=1).astype(BF16)
        sol = jnp.dot(ch["t"].astype(BF16), rhs, preferred_element_type=F32)
        u, w = sol[:, :HEAD_DIM], sol[:, HEAD_DIM:]
        wu = jnp.concatenate([w, u], axis=1).astype(BF16)
        kd = (k * jnp.exp(gtot - gcol)).astype(BF16)
        if with_q:
            qkm = (ch["qk"] * ch["e"] * incl_ref[d]).astype(BF16)
            r2 = jnp.dot(qkm, wu, preferred_element_type=F32)
            rows = pl.ds(ch["r0"], DN_SUPER)
            qe_ref[d, rows, :] = (ch["q"] * egc - r2[:, :HEAD_DIM]).astype(BF16)
            oacc_ref[d, rows, :] = r2[:, HEAD_DIM:]
        eg = jnp.exp(gtot)
        for c in range(per_super):
            cs = slice(c * DN_CHUNK, (c + 1) * DN_CHUNK)
            gb = lax.dot_general(kd[cs], wu[cs], (((0,), (0,)), ((), ())), preferred_element_type=F32)
            chunk = ch["sidx"] * per_super + c
            srows = pl.ds(pl.multiple_of(chunk * HEAD_DIM, HEAD_DIM), HEAD_DIM)
            g_ref[d, srows, :] = gb[:, :HEAD_DIM].astype(BF16)
            b_ref[d, srows, :] = gb[:, HEAD_DIM:]
            blk = jnp.broadcast_to(eg[c * DN_CHUNK:c * DN_CHUNK + 8, :], (8, HEAD_DIM))
            eg_ref[d, pl.ds(pl.multiple_of(chunk * 8, 8), 8), :] = blk

    group = 2 if n_super % 2 == 0 else 1

    def super_group(gi, carry):
        chains = [ch for s in range(group) for ch in chain_setup(gi * group + s)]
        for m in range(1, N_LEVELS):
            for ch in chains:
                ch["tb"] = ch["t"].astype(BF16)
                ch["x"] = jnp.dot((ch["nf"] * lev_ref[ch["d"], m]).astype(BF16), ch["tb"],
                                  preferred_element_type=F32)
            for ch in chains:
                ch["t"] = ch["t"] - jnp.dot(ch["tb"], ch["x"].astype(BF16), preferred_element_type=F32)
        for ch in chains:
            chain_finish(ch)
        return carry

    lax.fori_loop(0, n_super // group, super_group, 0)

    def chunk_step(c, states):
        ccs = (c, n_chunks - 1 - c)
        prods = []
        for d in range(2):
            srows = pl.ds(pl.multiple_of(ccs[d] * HEAD_DIM, HEAD_DIM), HEAD_DIM)
            lhs = g_ref[d, srows, :]
            if with_q:
                rows = pl.ds(pl.multiple_of(ccs[d] * DN_CHUNK, DN_CHUNK), DN_CHUNK)
                lhs = jnp.concatenate([lhs, qe_ref[d, rows, :]], axis=0)
            prods.append(jnp.dot(lhs, states[d].astype(BF16), preferred_element_type=F32))
        new_states = []
        for d in range(2):
            srows = pl.ds(pl.multiple_of(ccs[d] * HEAD_DIM, HEAD_DIM), HEAD_DIM)
            decay = eg_ref[d, pl.ds(pl.multiple_of(ccs[d] * 8, 8), 8), :][0:1, :]
            new_states.append(decay * states[d] - prods[d][:HEAD_DIM] + b_ref[d, srows, :])
            if with_q:
                rows = pl.ds(pl.multiple_of(ccs[d] * DN_CHUNK, DN_CHUNK), DN_CHUNK)
                oacc_ref[d, rows, :] += prods[d][HEAD_DIM:]
        return tuple(new_states)

    s_f, s_b = lax.fori_loop(0, n_chunks, chunk_step, (s0_ref[0], s0_ref[1]))
    sfin_ref[0] = s_f
    sfin_ref[1] = s_b
    if with_q:
        o = oacc_ref[0] + oacc_ref[1]
        on = o * lax.rsqrt(jnp.mean(o * o, axis=-1, keepdims=True) + EPS) * nw_ref[...]
        o_ref[...] = (on * _silu(z_ref[...])).astype(o_ref.dtype)
    else:
        o_ref[...] = jnp.zeros(o_ref.shape, o_ref.dtype)


def head_gate_layout(p2d):
    m = p2d.shape[0]
    c = COL_AB
    ab = p2d[:, c:c + 4 * DN_HEADS].astype(F32).reshape(m, 4, DN_HEADS)
    ab = jnp.transpose(ab, (0, 2, 1))
    ab = jnp.pad(ab, ((0, 0), (0, 0), (0, LANES - 4)))
    return ab.reshape(m, DN_HEADS * LANES)


def deltanet_mix(p2d, a_log, dt_bias, conv_w, norm_w, s0, batch, seq_len, with_q=True):
    incl, lev = dn_masks()
    ab = head_gate_layout(p2d)
    cols = (COL_QD, COL_KD, COL_VD, COL_ZD) if with_q else (COL_KD, COL_KD, COL_VD, COL_KD)
    cqb, ckb, cvb, czb = (c // HEAD_DIM for c in cols)
    seq_spec = lambda c0: pl.BlockSpec((seq_len, HEAD_DIM), lambda b, h: (b, c0 + h))
    conv_spec = lambda c0: pl.BlockSpec((DN_CONV, HEAD_DIM), lambda b, h: (0, c0 + h))
    state_spec = pl.BlockSpec((None, None, 2, HEAD_DIM, HEAD_DIM), lambda b, h: (b, h, 0, 0, 0))
    smem = pl.BlockSpec(memory_space=pltpu.SMEM)
    n_chunks = seq_len // DN_CHUNK
    per_dir = lambda dt: pltpu.VMEM((2, seq_len, HEAD_DIM), dt)
    pad = pltpu.VMEM((seq_len + 2 * CONV_PAD, HEAD_DIM), F32)
    return pl.pallas_call(
        functools.partial(_dn_kernel, seq_len=seq_len, with_q=with_q),
        out_shape=(jax.ShapeDtypeStruct((batch * seq_len, DN_WIDTH), BF16),
                   jax.ShapeDtypeStruct((batch, DN_HEADS, 2, HEAD_DIM, HEAD_DIM), F32)),
        grid=(batch, DN_HEADS),
        in_specs=[
            smem, smem,
            seq_spec(cqb), seq_spec(ckb), seq_spec(cvb), seq_spec(czb),
            pl.BlockSpec((seq_len, LANES), lambda b, h: (b, h)),
            conv_spec(0), conv_spec(DN_HEADS), conv_spec(2 * DN_HEADS),
            pl.BlockSpec((1, HEAD_DIM), lambda b, h: (0, 0)),
            state_spec,
            pl.BlockSpec((2, DN_SUPER, DN_SUPER), lambda b, h: (0, 0, 0)),
            pl.BlockSpec((2, N_LEVELS, DN_SUPER, DN_SUPER), lambda b, h: (0, 0, 0, 0)),
        ],
        out_specs=(pl.BlockSpec((seq_len, HEAD_DIM), lambda b, h: (b, h)), state_spec),
        scratch_shapes=[pad, pad, pad, per_dir(BF16),
                        pltpu.VMEM((2, n_chunks * HEAD_DIM, HEAD_DIM), BF16),
                        pltpu.VMEM((2, n_chunks * HEAD_DIM, HEAD_DIM), F32),
                        pltpu.VMEM((2, n_chunks * 8, HEAD_DIM), F32), per_dir(F32)],
        compiler_params=_params("parallel", "parallel"),
        name="deltanet_mix",
    )(a_log, dt_bias, p2d, p2d, p2d, p2d, ab, conv_w, conv_w, conv_w, norm_w.reshape(1, HEAD_DIM),
      s0, incl, lev)


def _merge_gate_kernel(y0, y1, y2, y3, g0, g1, g2, g3, wb_ref, o_ref):
    acc = None
    for n, (y_ref, g_ref) in enumerate(((y0, g0), (y1, g1), (y2, g2), (y3, g3))):
        t = jnp.dot(y_ref[...], wb_ref[n], preferred_element_type=F32) * jax.nn.sigmoid(g_ref[...].astype(F32))
        acc = t if acc is None else acc + t
    o_ref[...] = acc.astype(o_ref.dtype)


def merge_gate(branches, p2d, wb, tm=1024, tn=512):
    m = p2d.shape[0]
    d = wb.shape[-1]
    tm = min(tm, m)
    nj = d // tn
    y_spec = pl.BlockSpec((tm, BRANCH_WIDTH), lambda i, j: (i, 0))
    g_spec = lambda n: pl.BlockSpec((tm, tn), lambda i, j: (i, COL_GATE // tn + n * nj + j))
    return pl.pallas_call(
        _merge_gate_kernel,
        out_shape=jax.ShapeDtypeStruct((m, d), BF16),
        grid=(m // tm, nj),
        in_specs=[y_spec] * N_BRANCHES + [g_spec(n) for n in range(N_BRANCHES)]
        + [pl.BlockSpec((N_BRANCHES, BRANCH_WIDTH, tn), lambda i, j: (0, 0, j))],
        out_specs=pl.BlockSpec((tm, tn), lambda i, j: (i, j)),
        compiler_params=_params("parallel", "arbitrary"),
        name="merge_gate",
    )(*branches, p2d, p2d, p2d, p2d, wb)


def _out_proj_kernel(a_ref, w_ref, x_ref, g_ref, o_ref):
    o_ref[...] = x_ref[...] + g_ref[...] * jnp.dot(a_ref[...], w_ref[...], preferred_element_type=F32)


def out_project_residual(a, w, x2d, gate, rows_per_mod, tm=1024, tn=512):
    m, d = x2d.shape
    tm = min(tm, m)
    return pl.pallas_call(
        _out_proj_kernel,
        out_shape=jax.ShapeDtypeStruct((m, d), F32),
        grid=(m // tm, d // tn),
        in_specs=[
            pl.BlockSpec((tm, a.shape[1]), lambda i, j: (i, 0)),
            pl.BlockSpec((a.shape[1], tn), lambda i, j: (0, j)),
            pl.BlockSpec((tm, tn), lambda i, j: (i, j)),
            pl.BlockSpec((None, 1, tn), lambda i, j: ((i * tm) // rows_per_mod, 0, j)),
        ],
        out_specs=pl.BlockSpec((tm, tn), lambda i, j: (i, j)),
        compiler_params=_params("parallel", "arbitrary"),
        name="out_project_residual",
    )(a, w, x2d, gate)


def _first_argmax(v, lane):
    m = jnp.max(v, axis=-1, keepdims=True)
    idx = jnp.min(jnp.where(v == m, lane, LANES), axis=-1, keepdims=True)
    return m, idx


HI_MASK = 0xFFFF0000


def _pack_halves(v):
    half = v.shape[1] // 2
    lo = pltpu.bitcast(v[:, :half].astype(BF16).astype(F32), jnp.uint32)
    hi = pltpu.bitcast(v[:, half:].astype(BF16).astype(F32), jnp.uint32)
    return hi | (lo >> 16)


def _unpack_halves(p):
    return pltpu.bitcast(p << 16, F32), pltpu.bitcast(p & jnp.uint32(HI_MASK), F32)


ROW_CHUNKS = 16


def _store_row_blocked(ref, packed):
    rows = packed.shape[0]
    for c in range(ROW_CHUNKS):
        ref[pl.ds(c, rows, stride=ROW_CHUNKS), :] = packed[:, c * LANES:(c + 1) * LANES]


def _load_row_blocked(ref, rows):
    return jnp.concatenate([ref[pl.ds(c, rows, stride=ROW_CHUNKS), :] for c in range(ROW_CHUNKS)], axis=1)


def _route_kernel(x_ref, g_ref, sc_ref, sh_ref, rw_ref, rb_ref, tri_ref, cnt0_ref,
                  h_ref, ids_ref, wts_ref, rank_ref, cnt_ref, run_ref):
    h = _modulated_norm(x_ref[...], g_ref[...], sc_ref[...], sh_ref[...])
    _store_row_blocked(h_ref, _pack_halves(h))
    logits = jnp.dot(h, rw_ref[...], preferred_element_type=F32, precision=lax.Precision.HIGHEST)
    scores = jax.nn.sigmoid(logits)
    lane = lax.broadcasted_iota(jnp.int32, scores.shape, 1)
    grp = lane // (N_EXPERTS // N_EXPERT_GROUPS)
    ninf = -jnp.inf
    vb = jnp.where(lane < N_EXPERTS, scores + rb_ref[...], ninf)
    gs = jnp.full(scores.shape, ninf, F32)
    for g in range(N_EXPERT_GROUPS):
        vg = jnp.where(grp == g, vb, ninf)
        m1, i1 = _first_argmax(vg, lane)
        m2 = jnp.max(jnp.where(lane == i1, ninf, vg), axis=-1, keepdims=True)
        gs = jnp.where(lane == g, m1 + m2, gs)
    allowed = jnp.zeros(scores.shape, F32)
    for _ in range(TOPK_GROUPS):
        _, gi = _first_argmax(gs, lane)
        gs = jnp.where(lane == gi, ninf, gs)
        allowed = jnp.where(grp == gi, 1.0, allowed)
    ve = jnp.where(allowed > 0, vb, ninf)
    ids = jnp.zeros(scores.shape, jnp.int32)
    wts = jnp.zeros(scores.shape, F32)
    sel = jnp.zeros(scores.shape, F32)
    hits = []
    for k in range(TOP_K):
        _, ei = _first_argmax(ve, lane)
        hit = lane == ei
        hits.append(hit)
        sk = jnp.sum(jnp.where(hit, scores, 0.0), axis=-1, keepdims=True)
        ids = jnp.where(lane == k, ei, ids)
        wts = jnp.where(lane == k, sk, wts)
        sel = jnp.where(hit, 1.0, sel)
        ve = jnp.where(hit, ninf, ve)
    ids_ref[...] = ids
    wts_ref[...] = wts / jnp.sum(wts, axis=-1, keepdims=True) * ROUTE_SCALE

    @pl.when(pl.program_id(0) == 0)
    def _():
        run_ref[...] = cnt0_ref[...]

    before = jnp.dot(tri_ref[...], sel.astype(BF16), preferred_element_type=F32) + run_ref[...]
    ranks = jnp.zeros(scores.shape, F32)
    for k in range(TOP_K):
        rk = jnp.sum(jnp.where(hits[k], before, 0.0), axis=-1, keepdims=True)
        ranks = jnp.where(lane == k, rk, ranks)
    rank_ref[...] = ranks.astype(jnp.int32)
    run_ref[...] += jnp.sum(sel, axis=0, keepdims=True)
    cnt_ref[...] = run_ref[...]


def route(x2d, norm_w, sc1p, sh, router_w, router_bias, rows_per_mod, cnt0, tm=256):
    m, d = x2d.shape
    tm = min(tm, m)
    rw = jnp.pad(router_w, ((0, 0), (0, LANES - N_EXPERTS)))
    rb = jnp.pad(router_bias, (0, LANES - N_EXPERTS)).reshape(1, LANES)
    tri = jnp.asarray(np.tril(np.ones((tm, tm), np.float32), -1), dtype=BF16)
    mod_idx = lambda i: ((i * tm) // rows_per_mod, 0, 0)
    tok_lanes = pl.BlockSpec((tm, LANES), lambda i: (i, 0))
    one_row = pl.BlockSpec((1, LANES), lambda i: (0, 0))
    return pl.pallas_call(
        _route_kernel,
        out_shape=(jax.ShapeDtypeStruct((m * ROW_CHUNKS, LANES), jnp.uint32),
                   jax.ShapeDtypeStruct((m, LANES), jnp.int32),
                   jax.ShapeDtypeStruct((m, LANES), F32), jax.ShapeDtypeStruct((m, LANES), jnp.int32),
                   jax.ShapeDtypeStruct((1, LANES), F32)),
        grid=(m // tm,),
        in_specs=[
            pl.BlockSpec((tm, d), lambda i: (i, 0)),
            pl.BlockSpec((1, d), lambda i: (0, 0)),
            pl.BlockSpec((None, 1, d), mod_idx),
            pl.BlockSpec((None, 1, d), mod_idx),
            pl.BlockSpec((d, LANES), lambda i: (0, 0)),
            one_row,
            pl.BlockSpec((tm, tm), lambda i: (0, 0)),
            one_row,
        ],
        out_specs=(pl.BlockSpec((tm * ROW_CHUNKS, LANES), lambda i: (i, 0)), tok_lanes, tok_lanes, tok_lanes,
                   one_row),
        scratch_shapes=[pltpu.VMEM((1, LANES), F32)],
        compiler_params=_params("arbitrary"),
        name="route",
    )(x2d, norm_w.reshape(1, d), sc1p, sh, rw, rb, tri, cnt0)


def _half_k_dot(lo, hi, w_ref):
    half = lo.shape[1]
    return (jnp.dot(lo, w_ref[0:half, :], preferred_element_type=F32)
            + jnp.dot(hi, w_ref[half:, :], preferred_element_type=F32))


def _swiglu_packed(p, wg_ref, wu_ref, wd_ref):
    lo, hi = (v.astype(BF16) for v in _unpack_halves(p))
    hm = (_silu(_half_k_dot(lo, hi, wg_ref)) * _half_k_dot(lo, hi, wu_ref)).astype(BF16)
    return jnp.dot(hm, wd_ref[...], preferred_element_type=F32)


def _shared_ffn_kernel(h_ref, wg_ref, wu_ref, wd_ref, o_ref, acc_ref, wgb, wub, wdb):
    e = pl.program_id(1)
    wgb[...] = wg_ref[...].astype(BF16)
    wub[...] = wu_ref[...].astype(BF16)
    wdb[...] = wd_ref[...].astype(BF16)
    part = _swiglu_packed(_load_row_blocked(h_ref, acc_ref.shape[0]), wgb, wub, wdb)

    @pl.when(e == 0)
    def _():
        acc_ref[...] = part

    @pl.when(e > 0)
    def _():
        acc_ref[...] += part

    @pl.when(e == pl.num_programs(1) - 1)
    def _():
        o_ref[...] = acc_ref[...].astype(o_ref.dtype)


def shared_expert(h2p, wg, wu, wd, layer, tm=512):
    m = h2p.shape[0] // ROW_CHUNKS
    _, d, f = wg.shape
    tm = min(tm, m)
    fe = EXPERT_DIM
    return pl.pallas_call(
        _shared_ffn_kernel,
        out_shape=jax.ShapeDtypeStruct((m, d), BF16),
        grid=(m // tm, f // fe),
        in_specs=[
            pl.BlockSpec((tm * ROW_CHUNKS, LANES), lambda i, e: (i, 0)),
            pl.BlockSpec((None, d, fe), lambda i, e: (layer, 0, e)),
            pl.BlockSpec((None, d, fe), lambda i, e: (layer, 0, e)),
            pl.BlockSpec((None, fe, d), lambda i, e: (layer, e, 0)),
        ],
        out_specs=pl.BlockSpec((tm, d), lambda i, e: (i, 0)),
        scratch_shapes=[pltpu.VMEM((tm, d), F32), pltpu.VMEM((d, fe), BF16), pltpu.VMEM((d, fe), BF16),
                        pltpu.VMEM((fe, d), BF16)],
        compiler_params=_params("parallel", "arbitrary"),
        name="shared_expert",
    )(h2p, wg, wu, wd)


EXPERT_TILE = 256
COMBINE_TILE = 64


def dispatch_plan(ids, ranks, counts):
    n = ids.shape[0]
    a = n * TOP_K
    tm = EXPERT_TILE
    n_tiles = a // tm + N_EXPERTS
    tiles_per = (counts + tm - 1) // tm
    tile_end = jnp.cumsum(tiles_per)
    row_start = (tile_end - tiles_per) * tm
    experts = jnp.arange(N_EXPERTS, dtype=jnp.int32)
    row_of = ranks + jnp.sum(jnp.where(ids[..., None] == experts, row_start, 0), axis=-1)
    tok = jnp.arange(a, dtype=jnp.int32) // TOP_K
    row_tok = jnp.zeros((n_tiles * tm,), jnp.int32).at[row_of.reshape(a)].set(tok)
    n_used = tile_end[-1:]
    tiles = jnp.minimum(jnp.arange(n_tiles, dtype=jnp.int32), n_used[0] - 1)
    tile_e = jnp.sum((tile_end[None, :] <= tiles[:, None]).astype(jnp.int32), axis=1)
    tile_e = jnp.minimum(tile_e, N_EXPERTS - 1)
    return row_tok.reshape(n_tiles, 1, tm), tile_e, n_used.astype(jnp.int32), row_of.reshape(n, TOP_K)


def _row_block(r):
    return pl.ds(pl.multiple_of(r * ROW_CHUNKS, ROW_CHUNKS), ROW_CHUNKS)


def _expert_kernel(te_ref, nu_ref, cur_ref, nxt_ref, h_ref, wg_ref, wu_ref, wd_ref, o_ref,
                   xbuf, wgb, wub, wdb, sem):
    j = pl.program_id(0)
    n_used = nu_ref[0]
    slot = j % 2
    tm = xbuf.shape[1] // ROW_CHUNKS

    def gather(idx_ref, s):
        def body(r, carry):
            t = idx_ref[0, r]
            pltpu.make_async_copy(h_ref.at[_row_block(t), :], xbuf.at[s, _row_block(r), :], sem.at[s]).start()
            return carry
        lax.fori_loop(0, tm, body, 0, unroll=8)

    @pl.when(j == 0)
    def _():
        gather(cur_ref, 0)

    @pl.when(j + 1 < n_used)
    def _():
        gather(nxt_ref, 1 - slot)

    @pl.when(j < n_used)
    def _():
        @pl.when((j == 0) | (te_ref[j] != te_ref[jnp.maximum(j - 1, 0)]))
        def _():
            wgb[...] = wg_ref[...].astype(BF16)
            wub[...] = wu_ref[...].astype(BF16)
            wdb[...] = wd_ref[...].astype(BF16)

        pltpu.make_async_copy(xbuf.at[slot], xbuf.at[slot], sem.at[slot]).wait()
        x = _load_row_blocked(xbuf.at[slot], tm)
        _store_row_blocked(o_ref, _pack_halves(_swiglu_packed(x, wgb, wub, wdb)))

    @pl.when(j >= n_used)
    def _():
        o_ref[...] = jnp.zeros(o_ref.shape, o_ref.dtype)


def routed_experts(h2p, row_tok, tile_e, n_used, wg, wu, wd, layer):
    n_tiles, _, tm = row_tok.shape
    _, ne, d, f = wg.shape
    assert d == 2 * ROW_CHUNKS * LANES
    last = n_tiles - 1
    idx_spec = lambda shift: pl.BlockSpec((None, 1, tm), lambda j, te, nu: (jnp.minimum(j + shift, last), 0, 0),
                                          memory_space=pltpu.SMEM)
    return pl.pallas_call(
        _expert_kernel,
        out_shape=jax.ShapeDtypeStruct((n_tiles * tm * ROW_CHUNKS, LANES), jnp.uint32),
        grid_spec=pltpu.PrefetchScalarGridSpec(
            num_scalar_prefetch=2,
            grid=(n_tiles,),
            in_specs=[
                idx_spec(0), idx_spec(1),
                pl.BlockSpec(memory_space=pl.ANY),
                pl.BlockSpec((None, None, d, f), lambda j, te, nu: (layer, te[j], 0, 0)),
                pl.BlockSpec((None, None, d, f), lambda j, te, nu: (layer, te[j], 0, 0)),
                pl.BlockSpec((None, None, f, d), lambda j, te, nu: (layer, te[j], 0, 0)),
            ],
            out_specs=pl.BlockSpec((tm * ROW_CHUNKS, LANES), lambda j, te, nu: (j, 0)),
            scratch_shapes=[pltpu.VMEM((2, tm * ROW_CHUNKS, LANES), jnp.uint32), pltpu.VMEM((d, f), BF16),
                            pltpu.VMEM((d, f), BF16), pltpu.VMEM((f, d), BF16), pltpu.SemaphoreType.DMA((2,))],
        ),
        compiler_params=_params("arbitrary"),
        name="routed_experts",
    )(tile_e, n_used, row_tok, row_tok, h2p, wg, wu, wd)


def _combine_kernel(cur_ref, nxt_ref, ys_ref, w_ref, ysh_ref, x_ref, g_ref, nf_ref, o_ref, gbuf, sem, *,
                    final_norm):
    i = pl.program_id(0)
    slot = i % 2
    tmc = x_ref.shape[0]

    def gather(idx_ref, s):
        def body(r, carry):
            dst_rows = pl.ds(pl.multiple_of(r * ROW_CHUNKS, ROW_CHUNKS), ROW_CHUNKS)
            for k in range(TOP_K):
                t = idx_ref[0, r * TOP_K + k]
                src_rows = pl.ds(pl.multiple_of(t * ROW_CHUNKS, ROW_CHUNKS), ROW_CHUNKS)
                pltpu.make_async_copy(ys_ref.at[src_rows, :], gbuf.at[s, k, dst_rows, :], sem.at[s]).start()
            return carry
        lax.fori_loop(0, tmc, body, 0)

    @pl.when(i == 0)
    def _():
        gather(cur_ref, 0)

    @pl.when(i + 1 < pl.num_programs(0))
    def _():
        gather(nxt_ref, 1 - slot)

    pltpu.make_async_copy(gbuf.at[slot], gbuf.at[slot], sem.at[slot]).wait()
    w = w_ref[...]
    wk = [w[:, k:k + 1] for k in range(TOP_K)]
    cols_lo, cols_hi = [], []
    for c in range(ROW_CHUNKS):
        acc_lo = acc_hi = None
        for k in range(TOP_K):
            lo, hi = _unpack_halves(gbuf[slot, k, pl.ds(c, tmc, stride=ROW_CHUNKS), :])
            acc_lo = wk[k] * lo if acc_lo is None else acc_lo + wk[k] * lo
            acc_hi = wk[k] * hi if acc_hi is None else acc_hi + wk[k] * hi
        cols_lo.append(acc_lo)
        cols_hi.append(acc_hi)
    y = jnp.concatenate(cols_lo + cols_hi, axis=1) + ysh_ref[...].astype(F32)
    x = x_ref[...] + g_ref[...] * y
    if final_norm:
        x = x * lax.rsqrt(jnp.mean(x * x, axis=-1, keepdims=True) + EPS) * nf_ref[...]
    o_ref[...] = x


def combine_residual(x2d, ysh, ys, row_of, wts, gate, norm_final, rows_per_mod, row0, final_norm):
    m, d = x2d.shape
    tmc = COMBINE_TILE
    n_steps = m // tmc
    b0 = row0 // tmc
    idx = row_of.reshape(-1, 1, tmc * TOP_K)
    idx_spec = lambda shift: pl.BlockSpec(
        (None, 1, tmc * TOP_K), lambda i: (b0 + jnp.minimum(i + shift, n_steps - 1), 0, 0), memory_space=pltpu.SMEM)
    return pl.pallas_call(
        functools.partial(_combine_kernel, final_norm=final_norm),
        out_shape=jax.ShapeDtypeStruct((m, d), F32),
        grid=(n_steps,),
        in_specs=[
            idx_spec(0), idx_spec(1),
            pl.BlockSpec(memory_space=pl.ANY),
            pl.BlockSpec((tmc, LANES), lambda i: (b0 + i, 0)),
            pl.BlockSpec((tmc, d), lambda i: (b0 + i, 0)),
            pl.BlockSpec((tmc, d), lambda i: (i, 0)),
            pl.BlockSpec((None, 1, d), lambda i: ((i * tmc) // rows_per_mod, 0, 0)),
            pl.BlockSpec((1, d), lambda i: (0, 0)),
        ],
        out_specs=pl.BlockSpec((tmc, d), lambda i: (i, 0)),
        scratch_shapes=[pltpu.VMEM((2, TOP_K, tmc * ROW_CHUNKS, LANES), jnp.uint32),
                        pltpu.SemaphoreType.DMA((2,))],
        compiler_params=_params("arbitrary"),
        name="combine_residual",
    )(idx, idx, ys, wts, ysh, x2d, gate, norm_final.reshape(1, d))


def kernel(x, c, ctx, c_ctx, ada_w, ada_b, norm_mix, norm_ffn, w_in, dn_conv, dn_a_log, dn_dt_bias, dn_norm, attn_sink, pool_w, pool_scale, w_branch, w_out, router_w, router_bias, exp_gate, exp_up, exp_down, shared_gate, shared_up, shared_down, norm_final):
    batch, seq_len, d = x.shape
    n_ctx = ctx.shape[1]
    depth = ada_w.shape[0]
    cond_rows = 16
    cond = jnp.zeros((cond_rows, d), F32).at[:batch].set(c).at[batch].set(c_ctx)
    mod_all = ada_modulation(cond, ada_w, ada_b).reshape(depth, cond_rows, 6, 1, d)
    cos_t, sin_t = rope_tables(seq_len)
    xl = x.reshape(batch * seq_len, d)
    xc = ctx.reshape(batch * n_ctx, d)
    zero_state = jnp.zeros((batch, DN_HEADS, 2, HEAD_DIM, HEAD_DIM), F32)
    w_perm = permute_projection_columns(w_in.astype(BF16))
    for i in range(depth):
        last = i == depth - 1
        sh1, sc1, g1, sh2, sc2, g2 = (mod_all[i, :, s] for s in range(6))
        lat = lambda m: m[:batch]
        cx = lambda m: m[batch:batch + 1]
        wb = w_branch[i].astype(BF16)
        wo = w_out[i].astype(BF16)
        n_ctx_cols = COLS_CTX if last else COLS_PAD
        p_ctx = norm_project(xc, norm_mix[i], 1.0 + cx(sc1), cx(sh1), w_perm, i, n_ctx_cols, batch * n_ctx)
        p_lat = norm_project(xl, norm_mix[i], 1.0 + lat(sc1), lat(sh1), w_perm, i, COLS_PAD, seq_len)

        y_dn_c, s_ctx = deltanet_mix(p_ctx, dn_a_log[i], dn_dt_bias[i], dn_conv[i], dn_norm[i], zero_state,
                                     batch, n_ctx, with_q=not last)
        if not last:
            branches_c = (fourier_mix(p_ctx, batch, n_ctx), y_dn_c,
                          pool_mix(p_ctx, pool_w[i], pool_scale[i], batch, n_ctx),
                          context_attention(p_ctx, attn_sink[i], batch, n_ctx))
            xc = out_project_residual(merge_gate(branches_c, p_ctx, wb), wo, xc, cx(g1), batch * n_ctx)

        y_dn, _ = deltanet_mix(p_lat, dn_a_log[i], dn_dt_bias[i], dn_conv[i], dn_norm[i], s_ctx, batch, seq_len)
        branches = (fourier_mix(p_lat, batch, seq_len), y_dn,
                    pool_mix(p_lat, pool_w[i], pool_scale[i], batch, seq_len),
                    latent_attention(p_lat, p_ctx, attn_sink[i], cos_t, sin_t, batch, seq_len, n_ctx))
        xl = out_project_residual(merge_gate(branches, p_lat, wb), wo, xl, lat(g1), seq_len)

        no_tokens = jnp.zeros((1, LANES), F32)
        h2p, ids, wts, ranks, counts = route(xl, norm_ffn[i], 1.0 + lat(sc2), lat(sh2), router_w[i],
                                             router_bias[i], seq_len, no_tokens)
        if not last:
            *routed_c, counts = route(xc, norm_ffn[i], 1.0 + cx(sc2), cx(sh2), router_w[i], router_bias[i],
                                      batch * n_ctx, counts)
            h2p, ids, wts, ranks = (jnp.concatenate(pair, axis=0)
                                    for pair in zip((h2p, ids, wts, ranks), routed_c))
        ysh = shared_expert(h2p, shared_gate, shared_up, shared_down, i)
        row_tok, tile_e, n_used, row_of = dispatch_plan(ids[:, :TOP_K], ranks[:, :TOP_K],
                                                        counts[0, :N_EXPERTS].astype(jnp.int32))
        ys = routed_experts(h2p, row_tok, tile_e, n_used, exp_gate, exp_up, exp_down, i)
        n_lat = batch * seq_len
        if not last:
            xc = combine_residual(xc, ysh, ys, row_of, wts, cx(g2), norm_final, batch * n_ctx, n_lat, False)
        xl = combine_residual(xl, ysh, ys, row_of, wts, lat(g2), norm_final, seq_len, 0, last)
    return xl.reshape(batch, seq_len, d)
```
